```python
import jax
import jax.numpy as jnp
from jax import lax

D_MODEL = 1024
BATCH = 8
SEQ = 4096
DEPTH = 2

GRID_W = 64
CTX_LEN = 256
HEAD_DIM = 64
ATT_HEADS = D_MODEL // 256
ATT_KV_HEADS = ATT_HEADS // 2
WIN_HEADS = D_MODEL // 256
WIN_KV_HEADS = WIN_HEADS // 2
RWKV_HEADS = D_MODEL // 128
WINDOW = 128
Q_BLOCK = 128
ROPE_THETA = 10000.0
DECAY_LORA = 64
ICLR_LORA = 64
GATE_LORA = 128
N_EXPERTS = 16
N_EXPERT_GROUPS = 4
GROUP_SCORE_K = 2
TOP_K = 2
EXPERT_FF = D_MODEL
EXPERT_BLOCK = 128
N_MOD = 6
NORM_EPS = 1e-6
GN_EPS = 64e-5

ATT_Q = ATT_HEADS * HEAD_DIM
ATT_KV = ATT_KV_HEADS * HEAD_DIM
WIN_Q = WIN_HEADS * HEAD_DIM
WIN_KV = WIN_KV_HEADS * HEAD_DIM
RWKV_WIDTH = RWKV_HEADS * HEAD_DIM
MIX_WIDTH = ATT_Q + WIN_Q + RWKV_WIDTH
RWKV_COLS = 3 * RWKV_WIDTH + 2 * DECAY_LORA + 2 * ICLR_LORA + GATE_LORA
IN_COLS = ATT_Q + 2 * ATT_KV + WIN_Q + 2 * WIN_KV + RWKV_COLS

kernel_name = 'hybrid_flow_gqa_swa_rwkv7_moe'


def _split(t, sizes):
    offs, acc = [], 0
    for s in sizes[:-1]:
        acc += s
        offs.append(acc)
    return jnp.split(t, offs, axis=-1)


def _rms_norm(x, g):
    xf = x.astype(jnp.float32)
    y = xf * lax.rsqrt(jnp.mean(xf * xf, axis=-1, keepdims=True) + NORM_EPS)
    return (y * g.astype(jnp.float32)).astype(x.dtype)


def _modulate(x, shift, scale):
    return x * (1 + scale) + shift


def _rope_tables(n_tokens):
    rows = n_tokens // GRID_W
    row = jnp.repeat(jnp.arange(rows, dtype=jnp.float32), GRID_W)
    col = jnp.tile(jnp.arange(GRID_W, dtype=jnp.float32), rows)
    n_freq = HEAD_DIM // 4
    inv = ROPE_THETA ** (-jnp.arange(n_freq, dtype=jnp.float32) / n_freq)
    ang = jnp.concatenate([row[:, None] * inv, col[:, None] * inv], axis=-1)
    return jnp.cos(ang), jnp.sin(ang)


def _apply_rope(x, cos, sin):
    c = cos[None, :, None, :].astype(x.dtype)
    s = sin[None, :, None, :].astype(x.dtype)
    x1, x2 = x[..., 0::2], x[..., 1::2]
    return jnp.stack([x1 * c - x2 * s, x1 * s + x2 * c], axis=-1).reshape(x.shape)


def _heads(t, n):
    return t.reshape(t.shape[:-1] + (n, HEAD_DIM))


def _group(q, n_kv):
    return q.reshape(q.shape[:-2] + (n_kv, q.shape[-2] // n_kv, HEAD_DIM))


def _attend(q, k, v, mask=None, sink=None):
    s = jnp.einsum('bqhgd,bkhd->bhgqk', q, k).astype(jnp.float32) * (HEAD_DIM ** -0.5)
    if mask is not None:
        s = jnp.where(mask, s, -jnp.inf)
    if sink is not None:
        col = jnp.broadcast_to(sink.astype(jnp.float32)[None, :, :, None, None], s.shape[:-1] + (1,))
        s = jnp.concatenate([s, col], axis=-1)
    p = jax.nn.softmax(s, axis=-1)
    if sink is not None:
        p = p[..., :-1]
    return jnp.einsum('bhgqk,bkhd->bqhgd', p.astype(v.dtype), v)


def _global_gqa(q, k, v, qc, kc, vc, q_g, k_g, cos, sin, ctx_out):
    B, S = q.shape[:2]
    q = _group(_apply_rope(_rms_norm(_heads(q, ATT_HEADS), q_g), cos, sin), ATT_KV_HEADS)
    k = _apply_rope(_rms_norm(_heads(k, ATT_KV_HEADS), k_g), cos, sin)
    kc = _rms_norm(_heads(kc, ATT_KV_HEADS), k_g)
    vc = _heads(vc, ATT_KV_HEADS)
    keys = jnp.concatenate([kc, k], axis=1)
    vals = jnp.concatenate([vc, _heads(v, ATT_KV_HEADS)], axis=1)
    nb = S // Q_BLOCK
    qb = jnp.moveaxis(q.reshape((B, nb, Q_BLOCK) + q.shape[2:]), 1, 0)
    o = lax.map(lambda qi: _attend(qi, keys, vals), qb)
    o = jnp.moveaxis(o, 0, 1).reshape(B, S, ATT_Q)
    o_c = None
    if ctx_out:
        qc = _group(_rms_norm(_heads(qc, ATT_HEADS), q_g), ATT_KV_HEADS)
        o_c = _attend(qc, kc, vc).reshape(B, kc.shape[1], ATT_Q)
    return o, o_c


def _window_gqa(q, k, v, qc, kc, vc, sink, cos, sin, ctx_out):
    B, S = q.shape[:2]
    n_ctx = kc.shape[1]
    q = _group(_apply_rope(_heads(q, WIN_HEADS), cos, sin), WIN_KV_HEADS)
    k = _apply_rope(_heads(k, WIN_KV_HEADS), cos, sin)
    v = _heads(v, WIN_KV_HEADS)
    kc, vc = _heads(kc, WIN_KV_HEADS), _heads(vc, WIN_KV_HEADS)
    sink = sink.reshape(WIN_KV_HEADS, WIN_HEADS // WIN_KV_HEADS)
    pad = ((0, 0), (Q_BLOCK, Q_BLOCK), (0, 0), (0, 0))
    kp, vp = jnp.pad(k, pad), jnp.pad(v, pad)
    nb = S // Q_BLOCK
    band = 3 * Q_BLOCK
    qb = jnp.moveaxis(q.reshape((B, nb, Q_BLOCK) + q.shape[2:]), 1, 0)
    ctx_mask = jnp.ones((Q_BLOCK, n_ctx), dtype=bool)

    def block(args):
        qi, n = args
        kb = lax.dynamic_slice_in_dim(kp, n * Q_BLOCK, band, axis=1)
        vb = lax.dynamic_slice_in_dim(vp, n * Q_BLOCK, band, axis=1)
        qpos = n * Q_BLOCK + jnp.arange(Q_BLOCK)
        kpos = (n - 1) * Q_BLOCK + jnp.arange(band)
        near = (jnp.abs(kpos[None, :] - qpos[:, None]) <= WINDOW) & (kpos >= 0)[None, :] & (kpos < S)[None, :]
        mask = jnp.concatenate([near, ctx_mask], axis=-1)
        return _attend(qi, jnp.concatenate([kb, kc], axis=1), jnp.concatenate([vb, vc], axis=1), mask, sink)

    o = lax.map(block, (qb, jnp.arange(nb)))
    o = jnp.moveaxis(o, 0, 1).reshape(B, S, WIN_Q)
    o_c = None
    if ctx_out:
        qc = _group(_heads(qc, WIN_HEADS), WIN_KV_HEADS)
        o_c = _attend(qc, kc, vc, sink=sink).reshape(B, n_ctx, WIN_Q)
    return o, o_c


def _centred_shift(t, w):
    tp = jnp.pad(t, ((0, 0), (1, 1), (0, 0)))
    return tp[:, :-2] * w[0] + tp[:, 1:-1] * w[1] + tp[:, 2:] * w[2]


def _wkv7(s0, decay, k, v, kk, a, r):
    emit = r is not None
    xs = [decay, k, v, kk, a] + ([r] if emit else [])
    xs = [jnp.moveaxis(z, 1, 0) for z in xs]

    def step(S, inp):
        w_t, k_t, v_t, kk_t, a_t = inp[:5]
        sa = jnp.einsum('bhvk,bhk->bhv', S, kk_t)
        S = S * w_t[:, :, None, :] - sa[..., None] * (kk_t * a_t)[:, :, None, :] + v_t[..., None] * k_t[:, :, None, :]
        y = jnp.einsum('bhvk,bhk->bhv', S, inp[5]) if emit else None
        return S, y

    S, ys = lax.scan(step, s0, xs)
    return S, (jnp.moveaxis(ys, 0, 1) if emit else None)


def _bidir_wkv7(s0_f, s0_b, decay, kdir, v, kk, a, r):
    flip = lambda z: z[:, ::-1]
    s_f, o_f = _wkv7(s0_f, decay[:, :, 0], kdir[:, :, 0], v, kk, a[:, :, 0], r)
    s_b, o_b = _wkv7(s0_b, flip(decay[:, :, 1]), flip(kdir[:, :, 1]), flip(v), flip(kk), flip(a[:, :, 1]),
                     None if r is None else flip(r))
    out = None if r is None else o_f + flip(o_b)
    return s_f, s_b, out


def _rwkv7(u, uc, conv_w, w0, w2, a0, a2, g2, k_k, k_a, r_k, ln_w, ln_b, ctx_out):
    out_dtype = u.dtype
    f32 = jnp.float32
    conv_w, w0, w2, a0, a2, g2 = (p.astype(f32) for p in (conv_w, w0, w2, a0, a2, g2))
    k_k, k_a, r_k, ln_w, ln_b = (p.astype(f32) for p in (k_k, k_a, r_k, ln_w, ln_b))
    sizes = (RWKV_WIDTH, RWKV_WIDTH, RWKV_WIDTH, 2 * DECAY_LORA, 2 * ICLR_LORA, GATE_LORA)
    hd = lambda z: _heads(z, RWKV_HEADS)

    def prep(t):
        t = _centred_shift(t.astype(f32), conv_w)
        r, k, v, wd, ad, gd = _split(t, sizes)
        lead = t.shape[:2]
        wd = wd.reshape(lead + (2, DECAY_LORA))
        ad = ad.reshape(lead + (2, ICLR_LORA))
        logw = -jax.nn.softplus(-(w0 + jnp.einsum('btjr,jrc->btjc', jnp.tanh(wd), w2))) - 0.5
        decay = jnp.exp(-jnp.exp(logw))
        a = jax.nn.sigmoid(a0 + jnp.einsum('btjr,jrc->btjc', ad, a2))
        kdir = k[:, :, None, :] * (1.0 + (a - 1.0) * k_a)
        kk = hd(k * k_k)
        kk = kk / jnp.maximum(jnp.linalg.norm(kk, axis=-1, keepdims=True), 1e-12)
        return hd(r), hd(v), kk, hd(decay), hd(a), hd(kdir), gd

    def output(o, r, v, kdir, gd):
        mu = jnp.mean(o, axis=-1, keepdims=True)
        var = jnp.mean(jnp.square(o - mu), axis=-1, keepdims=True)
        lead = o.shape[:2]
        on = ((o - mu) * lax.rsqrt(var + GN_EPS)).reshape(lead + (RWKV_WIDTH,)) * ln_w + ln_b
        bonus = jnp.sum(r[:, :, None] * kdir * r_k, axis=-1, keepdims=True).sum(axis=2) * v
        gate = jax.nn.sigmoid(gd) @ g2
        return ((on + bonus.reshape(lead + (RWKV_WIDTH,))) * gate).astype(out_dtype)

    zero = jnp.zeros((u.shape[0], RWKV_HEADS, HEAD_DIM, HEAD_DIM), f32)
    rc, vc, kkc, dc, ac, kdc, gdc = prep(uc)
    s_f, s_b, oc = _bidir_wkv7(zero, zero, dc, kdc, vc, kkc, ac, rc if ctx_out else None)
    o_ctx = output(oc, rc, vc, kdc, gdc) if ctx_out else None
    r, v, kk, dec, a, kdir, gd = prep(u)
    _, _, ol = _bidir_wkv7(s_f, s_b, dec, kdir, v, kk, a, r)
    return output(ol, r, v, kdir, gd), o_ctx


def _moe(xf, w_router, router_bias, wg, wu, wd):
    n_tok, d = xf.shape
    per_group = N_EXPERTS // N_EXPERT_GROUPS
    scores = jax.nn.sigmoid(jnp.dot(xf.astype(jnp.float32), w_router.astype(jnp.float32)))
    biased = (scores + router_bias.astype(jnp.float32)).reshape(n_tok, N_EXPERT_GROUPS, per_group)
    group_score = lax.top_k(biased, GROUP_SCORE_K)[0].sum(axis=-1)
    g_sel = jnp.argmax(group_score, axis=-1).astype(jnp.int32)
    in_group = jnp.take_along_axis(biased, g_sel[:, None, None], axis=1)[:, 0]
    local = lax.top_k(in_group, TOP_K)[1].astype(jnp.int32)
    expert = g_sel[:, None] * per_group + local
    gate = jnp.take_along_axis(scores, expert, axis=1)
    gate = gate / jnp.sum(gate, axis=-1, keepdims=True)
    n_asg = n_tok * TOP_K
    e_flat = expert.reshape(-1)
    tok_flat = jnp.repeat(jnp.arange(n_tok, dtype=jnp.int32), TOP_K)
    order = jnp.argsort(e_flat)
    e_sorted = e_flat[order]
    counts = jnp.zeros((N_EXPERTS,), jnp.int32).at[e_flat].add(1)
    padded = (counts + EXPERT_BLOCK - 1) // EXPERT_BLOCK * EXPERT_BLOCK
    start = jnp.cumsum(counts) - counts
    pend = jnp.cumsum(padded)
    pstart = pend - padded
    slot_sorted = pstart[e_sorted] + jnp.arange(n_asg, dtype=jnp.int32) - start[e_sorted]
    n_slots = -(-n_asg // EXPERT_BLOCK) * EXPERT_BLOCK + N_EXPERTS * EXPERT_BLOCK
    n_blocks = n_slots // EXPERT_BLOCK
    slot_tok = jnp.full((n_slots,), n_tok, jnp.int32).at[slot_sorted].set(tok_flat[order])
    blk_expert = jnp.minimum(
        jnp.searchsorted(pend, jnp.arange(n_blocks, dtype=jnp.int32) * EXPERT_BLOCK, side='right'),
        N_EXPERTS - 1).astype(jnp.int32)
    x_pad = jnp.concatenate([xf, jnp.zeros((1, d), xf.dtype)], axis=0)
    xb = x_pad[slot_tok].reshape(n_blocks, EXPERT_BLOCK, d)

    def expert_ffn(args):
        xi, e = args
        return (jax.nn.silu(xi @ wg[e]) * (xi @ wu[e])) @ wd[e]

    yb = lax.map(expert_ffn, (xb, blk_expert)).reshape(n_slots, d)
    slot = jnp.zeros((n_asg,), jnp.int32).at[order].set(slot_sorted).reshape(n_tok, TOP_K)
    return jnp.einsum('tkd,tk->td', yb[slot], gate.astype(xf.dtype))


def setup_inputs(seed: int = 0) -> dict:
    key = jax.random.key(seed)
    keys = iter(jax.random.split(key, 32))

    def normal(shape, scale):
        return jax.random.normal(next(keys), shape, jnp.float32) * scale

    L, D, C, E, F = DEPTH, D_MODEL, RWKV_WIDTH, N_EXPERTS, EXPERT_FF
    shift_taps = jnp.array([0.2, 0.6, 0.2], jnp.float32)[None, :, None]
    return {
        'x': normal((BATCH, SEQ, D), 1.0),
        'c': normal((BATCH, D), 1.0),
        'ctx': normal((BATCH, CTX_LEN, D), 1.0),
        'c_ctx': normal((D,), 1.0),
        'w_mod': normal((L, D, N_MOD * D), 0.5 * D ** -0.5),
        'b_mod': normal((L, N_MOD * D), 0.02),
        'norm_mix_g': 1.0 + normal((L, D), 0.02),
        'norm_ffn_g': 1.0 + normal((L, D), 0.02),
        'w_in': normal((L, D, IN_COLS), D ** -0.5),
        'q_norm_g': 1.0 + normal((L, HEAD_DIM), 0.02),
        'k_norm_g': 1.0 + normal((L, HEAD_DIM), 0.02),
        'sink_logit': normal((L, WIN_HEADS), 0.5),
        'rwkv_conv': shift_taps + normal((L, 3, RWKV_COLS), 0.05),
        'rwkv_w0': jax.random.uniform(next(keys), (L, 2, C), jnp.float32, -4.0, 1.0),
        'rwkv_w2': normal((L, 2, DECAY_LORA, C), 0.1),
        'rwkv_a0': normal((L, 2, C), 0.5),
        'rwkv_a2': normal((L, 2, ICLR_LORA, C), 0.1),
        'rwkv_g2': normal((L, GATE_LORA, C), GATE_LORA ** -0.5),
        'rwkv_k_k': 0.85 + normal((L, C), 0.05),
        'rwkv_k_a': 1.0 + normal((L, C), 0.05),
        'rwkv_r_k': normal((L, 2, RWKV_HEADS, HEAD_DIM), 0.1),
        'rwkv_ln_w': 1.0 + normal((L, C), 0.02),
        'rwkv_ln_b': normal((L, C), 0.02),
        'w_out': normal((L, MIX_WIDTH, D), MIX_WIDTH ** -0.5),
        'w_router': normal((D, E), D ** -0.5),
        'router_bias': normal((E,), 0.01),
        'e_gate': normal((L, E, D, F), D ** -0.5),
        'e_up': normal((L, E, D, F), D ** -0.5),
        'e_down': normal((L, E, F, D), F ** -0.5),
        'final_norm_g': 1.0 + normal((D,), 0.02),
    }


def reference(x, c, ctx, c_ctx, w_mod, b_mod, norm_mix_g, norm_ffn_g, w_in, q_norm_g, k_norm_g,
              sink_logit, rwkv_conv, rwkv_w0, rwkv_w2, rwkv_a0, rwkv_a2, rwkv_g2, rwkv_k_k, rwkv_k_a,
              rwkv_r_k, rwkv_ln_w, rwkv_ln_b, w_out, w_router, router_bias, e_gate, e_up, e_down,
              final_norm_g):
    B, S, D = x.shape
    n_ctx = ctx.shape[1]
    cos, sin = _rope_tables(S)
    in_sizes = (ATT_Q, ATT_KV, ATT_KV, WIN_Q, WIN_KV, WIN_KV, RWKV_COLS)
    silu_c = jax.nn.silu(c)
    silu_cc = jax.nn.silu(c_ctx)
    h, hc = x, ctx
    for l in range(DEPTH):
        ctx_out = l < DEPTH - 1
        m = jnp.split((silu_c @ w_mod[l] + b_mod[l])[:, None, :], N_MOD, axis=-1)
        mc = jnp.split(silu_cc @ w_mod[l] + b_mod[l], N_MOD, axis=-1)
        u = _split(_modulate(_rms_norm(h, norm_mix_g[l]), m[0], m[1]) @ w_in[l], in_sizes)
        uc = _split(_modulate(_rms_norm(hc, norm_mix_g[l]), mc[0], mc[1]) @ w_in[l], in_sizes)
        oa, oa_c = _global_gqa(u[0], u[1], u[2], uc[0], uc[1], uc[2], q_norm_g[l], k_norm_g[l],
                               cos, sin, ctx_out)
        ob, ob_c = _window_gqa(u[3], u[4], u[5], uc[3], uc[4], uc[5], sink_logit[l], cos, sin, ctx_out)
        oc, oc_c = _rwkv7(u[6], uc[6], rwkv_conv[l], rwkv_w0[l], rwkv_w2[l], rwkv_a0[l], rwkv_a2[l],
                          rwkv_g2[l], rwkv_k_k[l], rwkv_k_a[l], rwkv_r_k[l], rwkv_ln_w[l], rwkv_ln_b[l],
                          ctx_out)
        h = h + m[2] * (jnp.concatenate([oa, ob, oc], axis=-1) @ w_out[l])
        f = _modulate(_rms_norm(h, norm_ffn_g[l]), m[3], m[4]).reshape(B * S, D)
        if ctx_out:
            hc = hc + mc[2] * (jnp.concatenate([oa_c, ob_c, oc_c], axis=-1) @ w_out[l])
            fc = _modulate(_rms_norm(hc, norm_ffn_g[l]), mc[3], mc[4]).reshape(B * n_ctx, D)
            y = _moe(jnp.concatenate([f, fc], axis=0), w_router, router_bias, e_gate[l], e_up[l], e_down[l])
            h = h + m[5] * y[:B * S].reshape(B, S, D)
            hc = hc + mc[5] * y[B * S:].reshape(B, n_ctx, D)
        else:
            y = _moe(f, w_router, router_bias, e_gate[l], e_up[l], e_down[l])
            h = h + m[5] * y.reshape(B, S, D)
    return _rms_norm(h, final_norm_g)
```

```python
import functools

import numpy as np
import jax
import jax.numpy as jnp
from jax import lax
from jax.experimental import pallas as pl
from jax.experimental.pallas import tpu as pltpu

D_MODEL = 1024
GRID_W = 64
HEAD_DIM = 64
ATT_HEADS = 4
ATT_KV_HEADS = 2
RWKV_HEADS = 8
RWKV_WIDTH = RWKV_HEADS * HEAD_DIM
WINDOW = 128
Q_BLOCK = 128
ROPE_THETA = 10000.0
DECAY_LORA = 64
ICLR_LORA = 64
GATE_LORA = 128
N_EXPERTS = 16
N_EXPERT_GROUPS = 4
N_MOD = 6
NORM_EPS = 1e-6
GN_EPS = 64e-5
ATT_COLS = 1024
RWKV_COLS = 3 * RWKV_WIDTH + 2 * DECAY_LORA + 2 * ICLR_LORA + GATE_LORA

LANES = 128
VMEM_LIMIT = 56 * 1024 * 1024

F32 = jnp.float32
BF16 = jnp.bfloat16


def _cparams(*sem):
    return pltpu.CompilerParams(dimension_semantics=sem, vmem_limit_bytes=VMEM_LIMIT)


def _dot(a, b):
    return jnp.dot(a.astype(BF16), b.astype(BF16), preferred_element_type=F32)


def _dot_nt(a, b):
    return lax.dot_general(a.astype(BF16), b.astype(BF16), (((1,), (1,)), ((), ())),
                           preferred_element_type=F32)


def _dot_tn(a, b):
    return lax.dot_general(a.astype(BF16), b.astype(BF16), (((0,), (0,)), ((), ())),
                           preferred_element_type=F32)


def _split_hi_lo(a):
    hi = a.astype(BF16)
    lo = (a - hi.astype(F32)).astype(BF16)
    return hi, lo


def _dot_x3(a, b):
    ah, al = _split_hi_lo(a)
    bh, bl = _split_hi_lo(b)
    return (jnp.dot(ah, bh, preferred_element_type=F32) + jnp.dot(al, bh, preferred_element_type=F32)
            + jnp.dot(ah, bl, preferred_element_type=F32))


def _dot_exact_rhs(a, b_bf16):
    ah, al = _split_hi_lo(a)
    return jnp.dot(ah, b_bf16, preferred_element_type=F32) + jnp.dot(al, b_bf16, preferred_element_type=F32)


def _sigmoid(x):
    return 1.0 / (1.0 + jnp.exp(-x))


def _mod_kernel(c_ref, w_ref, b_ref, o_ref):
    c = c_ref[...]
    s = c * _sigmoid(c)
    o_ref[...] = _dot_x3(s, w_ref[...]) + b_ref[...]


def _modulation(cond, w_mod_l, b_mod_l):
    r, d = cond.shape
    n = w_mod_l.shape[1]
    tn = 1024
    return pl.pallas_call(
        _mod_kernel,
        grid=(n // tn,),
        in_specs=[pl.BlockSpec((r, d), lambda j: (0, 0)),
                  pl.BlockSpec((d, tn), lambda j: (0, j)),
                  pl.BlockSpec((1, tn), lambda j: (0, j))],
        out_specs=pl.BlockSpec((r, tn), lambda j: (0, j)),
        out_shape=jax.ShapeDtypeStruct((r, n), F32),
        compiler_params=_cparams("arbitrary"),
        name="modulation",
    )(cond, w_mod_l, b_mod_l.reshape(1, n))


def _proj_in_kernel(h_ref, mod_ref, g_ref, wa_ref, wr_ref, ua_ref, ur_ref):
    x = h_ref[...]
    y = x * lax.rsqrt(jnp.mean(x * x, axis=-1, keepdims=True) + NORM_EPS) * g_ref[...]
    y = (y * (1.0 + mod_ref[1:2, :]) + mod_ref[0:1, :]).astype(BF16)
    ua_ref[...] = jnp.dot(y, wa_ref[...], preferred_element_type=F32)
    ur_ref[...] = jnp.dot(y, wr_ref[...], preferred_element_type=F32)


def _proj_in(h, modsel, g, wa, wr, n_ctx):
    b, t, d = h.shape
    tm = min(256, n_ctx)
    nct = n_ctx // tm
    return pl.pallas_call(
        _proj_in_kernel,
        grid=(b, t // tm),
        in_specs=[pl.BlockSpec((None, tm, d), lambda bi, i: (bi, i, 0)),
                  pl.BlockSpec((None, None, N_MOD, d), lambda bi, i: (bi, jnp.where(i >= nct, 1, 0), 0, 0)),
                  pl.BlockSpec((1, d), lambda bi, i: (0, 0)),
                  pl.BlockSpec(wa.shape, lambda bi, i: (0, 0)),
                  pl.BlockSpec(wr.shape, lambda bi, i: (0, 0))],
        out_specs=[pl.BlockSpec((None, tm, ATT_COLS), lambda bi, i: (bi, i, 0)),
                   pl.BlockSpec((None, tm, RWKV_COLS), lambda bi, i: (bi, i, 0))],
        out_shape=[jax.ShapeDtypeStruct((b, t, ATT_COLS), F32),
                   jax.ShapeDtypeStruct((b, t, RWKV_COLS), F32)],
        compiler_params=_cparams("parallel", "parallel"),
        name="proj_in",
    )(h, modsel, g.reshape(1, d), wa, wr)


def _swap_halves(x):
    lane = lax.broadcasted_iota(jnp.int32, x.shape, 1)
    fwd = pltpu.roll(x, LANES - HEAD_DIM // 2, 1)
    bwd = pltpu.roll(x, HEAD_DIM // 2, 1)
    return jnp.where(lane % HEAD_DIM < HEAD_DIM // 2, fwd, bwd)


def _attn_prep_kernel(ua_ref, cos_ref, sin_ref, qg_ref, kg_ref, grp_ref, o_ref):
    cos, sin = cos_ref[...], sin_ref[...]
    grp = grp_ref[...]
    scale = HEAD_DIM ** -0.5

    def head_norm(x, g):
        ss = _dot_exact_rhs(x * x, grp) * (1.0 / HEAD_DIM)
        return x * lax.rsqrt(ss + NORM_EPS) * g

    def rope(x):
        return x * cos + _swap_halves(x) * sin

    def blk(c):
        return ua_ref[:, c * LANES:(c + 1) * LANES]

    def put(c, v):
        o_ref[:, c * LANES:(c + 1) * LANES] = v.astype(BF16)

    put(0, rope(head_norm(blk(0), qg_ref[...])) * scale)
    put(1, rope(head_norm(blk(1), qg_ref[...])) * scale)
    put(2, rope(head_norm(blk(2), kg_ref[...])))
    put(3, blk(3))
    put(4, rope(blk(4)) * scale)
    put(5, rope(blk(5)) * scale)
    put(6, rope(blk(6)))
    put(7, blk(7))


def _attn_prep(ua, cos_t, sin_t, qg, kg, grp, n_ctx):
    b, t, _ = ua.shape
    tt = min(256, n_ctx)
    return pl.pallas_call(
        _attn_prep_kernel,
        grid=(b, t // tt),
        in_specs=[pl.BlockSpec((None, tt, ATT_COLS), lambda bi, i: (bi, i, 0)),
                  pl.BlockSpec((tt, LANES), lambda bi, i: (i, 0)),
                  pl.BlockSpec((tt, LANES), lambda bi, i: (i, 0)),
                  pl.BlockSpec((1, LANES), lambda bi, i: (0, 0)),
                  pl.BlockSpec((1, LANES), lambda bi, i: (0, 0)),
                  pl.BlockSpec((LANES, LANES), lambda bi, i: (0, 0))],
        out_specs=pl.BlockSpec((None, tt, ATT_COLS), lambda bi, i: (bi, i, 0)),
        out_shape=jax.ShapeDtypeStruct((b, t, ATT_COLS), BF16),
        compiler_params=_cparams("parallel", "parallel"),
        name="attn_prep",
    )(ua, cos_t, sin_t, qg, kg, grp)


def _global_attn_kernel(q_ref, k_ref, v_ref, o_ref, m_scr, l_scr, acc_scr, *, tk, n_ctx_blocks, n_kv_ctx, n_kv_all):
    i = pl.program_id(1)
    tq = q_ref.shape[0]
    lane = lax.broadcasted_iota(jnp.int32, (tq, LANES), 1)
    low = lane < HEAD_DIM
    zero = jnp.zeros((), BF16)
    qs = []
    for j in range(4):
        blk = q_ref[:, (j // 2) * LANES:(j // 2 + 1) * LANES]
        qs.append(jnp.where(low if j % 2 == 0 else ~low, blk, zero))
    m_scr[...] = jnp.full(m_scr.shape, -jnp.inf, F32)
    l_scr[...] = jnp.zeros(l_scr.shape, F32)
    acc_scr[...] = jnp.zeros(acc_scr.shape, F32)
    n_kv = jnp.where(i < n_ctx_blocks, n_kv_ctx, n_kv_all)

    def body(c, carry):
        off = pl.multiple_of(c * tk, tk)
        kc = k_ref[pl.ds(off, tk), :]
        vc = v_ref[pl.ds(off, tk), :]
        for j in range(4):
            s = _dot_nt(qs[j], kc)
            m_old = m_scr[j]
            m_new = jnp.maximum(m_old, jnp.max(s, axis=-1, keepdims=True))
            p = jnp.exp(s - m_new)
            alpha = jnp.exp(m_old - m_new)
            l_scr[j] = alpha * l_scr[j] + jnp.sum(p, axis=-1, keepdims=True)
            acc_scr[j] = alpha * acc_scr[j] + jnp.dot(p.astype(BF16), vc, preferred_element_type=F32)
            m_scr[j] = m_new
        return carry

    lax.fori_loop(0, n_kv, body, 0)
    for blk in range(2):
        o_lo = acc_scr[2 * blk] / l_scr[2 * blk]
        o_hi = acc_scr[2 * blk + 1] / l_scr[2 * blk + 1]
        o_ref[:, blk * LANES:(blk + 1) * LANES] = jnp.where(low, o_lo, o_hi).astype(BF16)


def _global_attn(qkv, n_ctx):
    b, t, _ = qkv.shape
    tq = min(256, n_ctx)
    tk = min(256, n_ctx)
    kern = functools.partial(_global_attn_kernel, tk=tk, n_ctx_blocks=n_ctx // tq,
                             n_kv_ctx=n_ctx // tk, n_kv_all=t // tk)
    return pl.pallas_call(
        kern,
        grid=(b, t // tq),
        in_specs=[pl.BlockSpec((None, tq, 2 * LANES), lambda bi, i: (bi, i, 0)),
                  pl.BlockSpec((None, t, LANES), lambda bi, i: (bi, 0, 2)),
                  pl.BlockSpec((None, t, LANES), lambda bi, i: (bi, 0, 3))],
        out_specs=pl.BlockSpec((None, tq, 2 * LANES), lambda bi, i: (bi, i, 0)),
        out_shape=jax.ShapeDtypeStruct((b, t, 2 * LANES), BF16),
        scratch_shapes=[pltpu.VMEM((4, tq, 1), F32), pltpu.VMEM((4, tq, 1), F32),
                        pltpu.VMEM((4, tq, LANES), F32)],
        compiler_params=_cparams("parallel", "arbitrary"),
        name="global_attn",
    )(qkv, qkv, qkv)


def _window_attn_kernel(sink_ref, q_ref, k_ref, v_ref, o_ref, *, n_ctx, seq):
    i = pl.program_id(1)
    qb = Q_BLOCK
    band = 3 * qb
    n_ctx_blocks = n_ctx // qb
    is_lat = i >= n_ctx_blocks
    n = i - n_ctx_blocks
    start = jnp.clip((n - 1) * qb, 0, seq - band)
    off = pl.multiple_of(start + n_ctx, qb)
    kc, vc = k_ref[0:n_ctx, :], v_ref[0:n_ctx, :]
    kb, vb = k_ref[pl.ds(off, band), :], v_ref[pl.ds(off, band), :]
    qpos = n * qb + lax.broadcasted_iota(jnp.int32, (qb, band), 0)
    kpos = start + lax.broadcasted_iota(jnp.int32, (qb, band), 1)
    near = (jnp.abs(kpos - qpos) <= WINDOW) & is_lat
    lane = lax.broadcasted_iota(jnp.int32, (qb, LANES), 1)
    low = lane < HEAD_DIM
    zero = jnp.zeros((), BF16)
    outs = []
    for j in range(4):
        blk = q_ref[:, (j // 2) * LANES:(j // 2 + 1) * LANES]
        q = jnp.where(low if j % 2 == 0 else ~low, blk, zero)
        sink = sink_ref[j]
        s_c = _dot_nt(q, kc)
        s_b = jnp.where(near, _dot_nt(q, kb), -jnp.inf)
        m = jnp.maximum(jnp.maximum(jnp.max(s_c, axis=-1, keepdims=True), jnp.max(s_b, axis=-1, keepdims=True)), sink)
        p_c = jnp.exp(s_c - m)
        p_b = jnp.exp(s_b - m)
        den = jnp.sum(p_c, axis=-1, keepdims=True) + jnp.sum(p_b, axis=-1, keepdims=True) + jnp.exp(sink - m)
        acc = jnp.dot(p_c.astype(BF16), vc, preferred_element_type=F32) + jnp.dot(p_b.astype(BF16), vb, preferred_element_type=F32)
        outs.append(acc / den)
    for blk in range(2):
        o_ref[:, blk * LANES:(blk + 1) * LANES] = jnp.where(low, outs[2 * blk], outs[2 * blk + 1]).astype(BF16)


def _window_attn(qkv, sink, n_ctx):
    b, t, _ = qkv.shape
    kern = functools.partial(_window_attn_kernel, n_ctx=n_ctx, seq=t - n_ctx)
    return pl.pallas_call(
        kern,
        grid_spec=pltpu.PrefetchScalarGridSpec(
            num_scalar_prefetch=1,
            grid=(b, t // Q_BLOCK),
            in_specs=[pl.BlockSpec((None, Q_BLOCK, 2 * LANES), lambda bi, i, s: (bi, i, 2)),
                      pl.BlockSpec((None, t, LANES), lambda bi, i, s: (bi, 0, 6)),
                      pl.BlockSpec((None, t, LANES), lambda bi, i, s: (bi, 0, 7))],
            out_specs=pl.BlockSpec((None, Q_BLOCK, 2 * LANES), lambda bi, i, s: (bi, i, 0)),
        ),
        out_shape=jax.ShapeDtypeStruct((b, t, 2 * LANES), BF16),
        compiler_params=_cparams("parallel", "arbitrary"),
        name="window_attn",
    )(sink, qkv, qkv, qkv)


_HEAD_PERM = np.concatenate([np.arange(0, HEAD_DIM, 2), np.arange(1, HEAD_DIM, 2)])
_Q_HEAD_ORDER = (0, 2, 1, 3)


def _q_cols(base, rotary):
    inner = _HEAD_PERM if rotary else np.arange(HEAD_DIM)
    return np.concatenate([base + h * HEAD_DIM + inner for h in _Q_HEAD_ORDER])


def _kv_cols(base, rotary):
    inner = _HEAD_PERM if rotary else np.arange(HEAD_DIM)
    return np.concatenate([base + h * HEAD_DIM + inner for h in range(ATT_KV_HEADS)])


_ATT_COL_PERM = np.concatenate([
    _q_cols(0, True), _kv_cols(256, True), _kv_cols(384, False),
    _q_cols(512, True), _kv_cols(768, True), _kv_cols(896, False)])
_OUT_ROW_PERM = np.concatenate([_q_cols(0, False), _q_cols(256, False), np.arange(512, 1024)])


def _rope_tables(n_ctx, seq):
    rows = seq // GRID_W
    row = jnp.repeat(jnp.arange(rows, dtype=F32), GRID_W)
    col = jnp.tile(jnp.arange(GRID_W, dtype=F32), rows)
    n_freq = HEAD_DIM // 4
    inv = ROPE_THETA ** (-jnp.arange(n_freq, dtype=F32) / n_freq)
    ang = jnp.concatenate([row[:, None] * inv, col[:, None] * inv], axis=-1)
    cos = jnp.concatenate([jnp.ones((n_ctx, HEAD_DIM // 2), F32), jnp.cos(ang)], axis=0)
    sin = jnp.concatenate([jnp.zeros((n_ctx, HEAD_DIM // 2), F32), jnp.sin(ang)], axis=0)
    cos_t = jnp.tile(jnp.concatenate([cos, cos], axis=-1), (1, LANES // HEAD_DIM))
    sin_t = jnp.tile(jnp.concatenate([-sin, sin], axis=-1), (1, LANES // HEAD_DIM))
    return cos_t, sin_t


def _rwkv_prep_kernel(cur_ref, prev_ref, next_ref, conv_ref, w0_ref, w2_ref, a0_ref, a2_ref, g2_ref,
                      kk_ref, ka_ref, rk_ref, grp_ref,
                      r_out, v_out, al_out, lw_out, kd_out, be_out, bonus_out, gate_out, *, n_ctx_tiles, n_tiles):
    i = pl.program_id(1)
    tt = cur_ref.shape[0]
    c = RWKV_WIDTH
    x = cur_ref[...]
    row = lax.broadcasted_iota(jnp.int32, (tt, 1), 0)
    has_prev = jnp.logical_and(i != 0, i != n_ctx_tiles)
    has_next = jnp.logical_and(i != n_ctx_tiles - 1, i != n_tiles - 1)
    halo_prev = jnp.where(has_prev, prev_ref[7:8, :], 0.0)
    halo_next = jnp.where(has_next, next_ref[0:1, :], 0.0)
    x_prev = jnp.where(row == 0, halo_prev, pltpu.roll(x, 1, 0))
    x_next = jnp.where(row == tt - 1, halo_next, pltpu.roll(x, tt - 1, 0))
    t = x_prev * conv_ref[0:1, :] + x * conv_ref[1:2, :] + x_next * conv_ref[2:3, :]
    r, k, v = t[:, 0:c], t[:, c:2 * c], t[:, 2 * c:3 * c]
    wd = t[:, 3 * c:3 * c + 2 * DECAY_LORA]
    ad = t[:, 3 * c + 2 * DECAY_LORA:3 * c + 2 * DECAY_LORA + 2 * ICLR_LORA]
    gd = t[:, 3 * c + 2 * DECAY_LORA + 2 * ICLR_LORA:]
    z = w0_ref[...] + _dot_x3(jnp.tanh(wd), w2_ref[...])
    neg_softplus = -(jnp.maximum(-z, 0.0) + jnp.log(1.0 + jnp.exp(-jnp.abs(z))))
    lw_out[...] = -jnp.exp(neg_softplus - 0.5)
    a = _sigmoid(a0_ref[...] + _dot_x3(ad, a2_ref[...]))
    grp = grp_ref[...]
    kk = k * kk_ref[...]
    ss = jnp.concatenate([_dot_exact_rhs(kk[:, j * LANES:(j + 1) * LANES] * kk[:, j * LANES:(j + 1) * LANES], grp)
                          for j in range(c // LANES)], axis=-1)
    kk = kk / jnp.maximum(jnp.sqrt(ss), 1e-12)
    bonus_dot = jnp.zeros((tt, c), F32)
    for j in range(2):
        a_j = a[:, j * c:(j + 1) * c]
        kd_j = k * (1.0 + (a_j - 1.0) * ka_ref[...])
        kd_out[:, j * c:(j + 1) * c] = kd_j
        be_out[:, j * c:(j + 1) * c] = a_j * kk
        bonus_dot = bonus_dot + r * kd_j * rk_ref[j:j + 1, :]
    bsum = jnp.concatenate([_dot_exact_rhs(bonus_dot[:, j * LANES:(j + 1) * LANES], grp)
                            for j in range(c // LANES)], axis=-1)
    r_out[...] = r
    v_out[...] = v
    al_out[...] = kk
    bonus_out[...] = bsum * v
    gate_out[...] = _dot_x3(_sigmoid(gd), g2_ref[...])


def _rwkv_prep(ur, conv, w0, w2blk, a0, a2blk, g2, k_k, k_a, r_k, grp, n_ctx):
    b, t, cols = ur.shape
    c = RWKV_WIDTH
    tt = min(256, n_ctx)
    n_tiles = t // tt
    per8 = tt // 8
    kern = functools.partial(_rwkv_prep_kernel, n_ctx_tiles=n_ctx // tt, n_tiles=n_tiles)
    const = lambda shape: pl.BlockSpec(shape, lambda bi, i: (0,) * len(shape))
    tok = lambda w: pl.BlockSpec((None, tt, w), lambda bi, i: (bi, i, 0))
    return pl.pallas_call(
        kern,
        grid=(b, n_tiles),
        in_specs=[tok(cols),
                  pl.BlockSpec((None, 8, cols), lambda bi, i: (bi, jnp.maximum(i * per8 - 1, 0), 0)),
                  pl.BlockSpec((None, 8, cols), lambda bi, i: (bi, jnp.minimum((i + 1) * per8, t // 8 - 1), 0)),
                  const((3, cols)), const((1, 2 * c)), const((2 * DECAY_LORA, 2 * c)), const((1, 2 * c)),
                  const((2 * ICLR_LORA, 2 * c)), const((GATE_LORA, c)), const((1, c)), const((1, c)),
                  const((2, c)), const((LANES, LANES))],
        out_specs=[tok(c), tok(c), tok(c), tok(2 * c), tok(2 * c), tok(2 * c), tok(c), tok(c)],
        out_shape=[jax.ShapeDtypeStruct((b, t, w), F32) for w in (c, c, c, 2 * c, 2 * c, 2 * c, c, c)],
        compiler_params=_cparams("parallel", "parallel"),
        name="rwkv_prep",
    )(ur, ur, ur, conv, w0, w2blk, a0, a2blk, g2, k_k, k_a, r_k, grp)


RWKV_CHUNK = 64
INV_BLOCK = 16


def _rwkv_scan_kernel(r_ref, v_ref, al_ref, lw_ref, kd_ref, be_ref, o_ref, s_scr):
    rev = pl.program_id(1) == 1
    step = pl.program_id(2)
    cs = RWKV_CHUNK
    n2 = 2 * cs

    @pl.when(step == 0)
    def _():
        s_scr[...] = jnp.zeros(s_scr.shape, F32)

    rr = lax.broadcasted_iota(jnp.int32, (n2, n2), 0)
    cc = lax.broadcasted_iota(jnp.int32, (n2, n2), 1)
    sign = jnp.where(rev, -1, 1)
    same_head = (rr // cs) == (cc // cs)
    delta = (cc - rr) * sign
    strict = same_head & (delta < 0)
    incl = same_head & (delta <= 0)
    inv_blk = (rr // INV_BLOCK) == (cc // INV_BLOCK)
    eye = rr == cc
    tr = lax.broadcasted_iota(jnp.int32, (cs, cs), 0)
    tc = lax.broadcasted_iota(jnp.int32, (cs, cs), 1)
    tri = jnp.where((tc - tr) * sign <= 0, 1.0, 0.0).astype(BF16)
    low = lax.broadcasted_iota(jnp.int32, (cs, LANES), 1) < HEAD_DIM

    lw = lw_ref[...]
    lw_hi, lw_lo = _split_hi_lo(lw)
    cl = jnp.dot(tri, lw_hi, preferred_element_type=F32) + jnp.dot(tri, lw_lo, preferred_element_type=F32)
    tot = jnp.sum(lw, axis=0, keepdims=True)
    e_in, e_ex = jnp.exp(cl), jnp.exp(cl - lw)
    e_neg, e_end = jnp.exp(-cl), jnp.exp(tot - cl)
    p_end = jnp.exp(tot)
    a_t = al_ref[...] * e_ex
    r_t = r_ref[...] * e_in
    k_t = kd_ref[...] * e_neg
    b_t = be_ref[...] * e_neg
    k_e = kd_ref[...] * e_end
    b_e = be_ref[...] * e_end
    v = v_ref[...]

    def stack(x):
        return jnp.concatenate([jnp.where(low, x, 0.0), jnp.where(low, 0.0, x)], axis=0)

    def twice(x):
        return jnp.concatenate([x, x], axis=0)

    for p in range(RWKV_HEADS // 2):
        sl = slice(p * LANES, (p + 1) * LANES)
        xa, xr = stack(a_t[:, sl]), stack(r_t[:, sl])
        yb, yk = twice(b_t[:, sl]), twice(k_t[:, sl])
        l_m = jnp.where(strict, _dot_nt(xa, yb), 0.0)
        m_ak = jnp.where(strict, _dot_nt(xa, yk), 0.0)
        m_rb = jnp.where(incl, _dot_nt(xr, yb), 0.0)
        m_rk = jnp.where(incl, _dot_nt(xr, yk), 0.0)
        l_d = jnp.where(inv_blk, l_m, 0.0)
        l_o = l_m - l_d
        l2 = _dot(l_d, l_d)
        l4 = _dot(l2, l2)
        l8 = _dot(l4, l4)
        d_inv = _dot(_dot(jnp.where(eye, 1.0, 0.0) - l_d, jnp.where(eye, 1.0, 0.0) + l2),
                     _dot(jnp.where(eye, 1.0, 0.0) + l4, jnp.where(eye, 1.0, 0.0) + l8))
        n_m = _dot(d_inv, l_o)
        n2_m = _dot(n_m, n_m)
        t_m = _dot(jnp.where(eye, 1.0, 0.0) - n_m + n2_m - _dot(n_m, n2_m), d_inv)
        vs = stack(v[:, sl])
        u0 = _dot(t_m, _dot(m_ak, vs))
        w1 = _dot(t_m, xa)
        y0s = _dot(m_rk, vs) - _dot(m_rb, u0)
        rqs = xr - _dot(m_rb, w1)
        y0 = y0s[:cs] + y0s[cs:]
        rq = rqs[:cs] + rqs[cs:]
        ks, bs = stack(k_e[:, sl]), stack(b_e[:, sl])
        g_t = _dot_tn(ks, vs) - _dot_tn(bs, u0)
        m_t = jnp.where(eye, p_end[:, sl], 0.0) - _dot_tn(bs, w1)
        s_old = s_scr[p]
        o_ref[:, sl] = _dot(rq, s_old) + y0
        s_scr[p] = _dot(m_t, s_old) + g_t


def _rwkv_scan(r, v, al, lw, kd, be, n_ctx):
    b, t, c = r.shape
    cs = RWKV_CHUNK
    ncc, ntot = n_ctx // cs, t // cs

    def chunk(d, s):
        back = jnp.where(s < ncc, ncc - 1 - s, ntot - 1 - s + ncc)
        return jnp.where(d == 0, s, back)

    shared = pl.BlockSpec((None, cs, c), lambda bi, d, s: (bi, chunk(d, s), 0))
    per_dir = pl.BlockSpec((None, cs, c), lambda bi, d, s: (bi, chunk(d, s), d))
    return pl.pallas_call(
        _rwkv_scan_kernel,
        grid=(b, 2, ntot),
        in_specs=[shared, shared, shared, per_dir, per_dir, per_dir],
        out_specs=pl.BlockSpec((None, None, cs, c), lambda bi, d, s: (bi, d, chunk(d, s), 0)),
        out_shape=jax.ShapeDtypeStruct((b, 2, t, c), F32),
        scratch_shapes=[pltpu.VMEM((RWKV_HEADS // 2, LANES, LANES), F32)],
        compiler_params=_cparams("parallel", "parallel", "arbitrary"),
        name="rwkv_scan",
    )(r, v, al, lw, kd, be)


def _rwkv_out_kernel(of_ref, ob_ref, bonus_ref, gate_ref, lnw_ref, lnb_ref, grp_ref, o_ref):
    grp = grp_ref[...]
    for j in range(RWKV_WIDTH // LANES):
        sl = slice(j * LANES, (j + 1) * LANES)
        o = of_ref[:, sl] + ob_ref[:, sl]
        mu = _dot_exact_rhs(o, grp) * (1.0 / HEAD_DIM)
        d = o - mu
        var = _dot_exact_rhs(d * d, grp) * (1.0 / HEAD_DIM)
        on = d * lax.rsqrt(var + GN_EPS) * lnw_ref[:, sl] + lnb_ref[:, sl]
        o_ref[:, sl] = ((on + bonus_ref[:, sl]) * gate_ref[:, sl]).astype(BF16)


def _rwkv_out(o2, bonus, gate, ln_w, ln_b, grp, n_ctx):
    b, _, t, c = o2.shape
    tt = min(256, n_ctx)
    tok = pl.BlockSpec((None, tt, c), lambda bi, i: (bi, i, 0))
    const = lambda shape: pl.BlockSpec(shape, lambda bi, i: (0,) * len(shape))
    return pl.pallas_call(
        _rwkv_out_kernel,
        grid=(b, t // tt),
        in_specs=[pl.BlockSpec((None, None, tt, c), lambda bi, i: (bi, 0, i, 0)),
                  pl.BlockSpec((None, None, tt, c), lambda bi, i: (bi, 1, i, 0)),
                  tok, tok, const((1, c)), const((1, c)), const((LANES, LANES))],
        out_specs=tok,
        out_shape=jax.ShapeDtypeStruct((b, t, c), BF16),
        compiler_params=_cparams("parallel", "parallel"),
        name="rwkv_out",
    )(o2, o2, bonus, gate, ln_w, ln_b, grp)


def _block_diag2(w):
    z = jnp.zeros_like(w[0])
    return jnp.concatenate([jnp.concatenate([w[0], z], axis=1), jnp.concatenate([z, w[1]], axis=1)], axis=0)


def _proj_out_kernel(h_ref, oa_ref, ow_ref, or_ref, w_ref, mod_ref, g_ref, hn_ref, f_ref):
    na, nw = oa_ref.shape[1], ow_ref.shape[1]
    mix = (jnp.dot(oa_ref[...], w_ref[0:na, :], preferred_element_type=F32)
           + jnp.dot(ow_ref[...], w_ref[na:na + nw, :], preferred_element_type=F32)
           + jnp.dot(or_ref[...], w_ref[na + nw:, :], preferred_element_type=F32))
    h = h_ref[...] + mod_ref[2:3, :] * mix
    hn_ref[...] = h
    y = h * lax.rsqrt(jnp.mean(h * h, axis=-1, keepdims=True) + NORM_EPS) * g_ref[...]
    f_ref[...] = y * (1.0 + mod_ref[4:5, :]) + mod_ref[3:4, :]


def _proj_out(h, o_a, o_w, o_r, w_out, modsel, g, n_ctx):
    b, t, d = h.shape
    tm = min(256, n_ctx)
    nct = n_ctx // tm
    tok = lambda w: pl.BlockSpec((None, tm, w), lambda bi, i: (bi, i, 0))
    return pl.pallas_call(
        _proj_out_kernel,
        grid=(b, t // tm),
        in_specs=[tok(d), tok(o_a.shape[2]), tok(o_w.shape[2]), tok(o_r.shape[2]),
                  pl.BlockSpec(w_out.shape, lambda bi, i: (0, 0)),
                  pl.BlockSpec((None, None, N_MOD, d), lambda bi, i: (bi, jnp.where(i >= nct, 1, 0), 0, 0)),
                  pl.BlockSpec((1, d), lambda bi, i: (0, 0))],
        out_specs=[tok(d), tok(d)],
        out_shape=[jax.ShapeDtypeStruct((b, t, d), F32), jax.ShapeDtypeStruct((b, t, d), F32)],
        compiler_params=_cparams("parallel", "parallel"),
        name="proj_out",
    )(h, o_a, o_w, o_r, w_out, modsel, g.reshape(1, d))


def _router_kernel(f_ref, wr_ref, bias_ref, e_ref, gate_ref, rank_ref, cnt_ref):
    tt = f_ref.shape[0]
    per_group = N_EXPERTS // N_EXPERT_GROUPS

    @pl.when(pl.program_id(0) == 0)
    def _():
        cnt_ref[...] = jnp.zeros(cnt_ref.shape, F32)

    logits = _dot_nt(wr_ref[...], f_ref[...])
    score = _sigmoid(logits)
    biased = score + bias_ref[...]
    b = [biased[e:e + 1, :] for e in range(N_EXPERTS)]
    s = [score[e:e + 1, :] for e in range(N_EXPERTS)]
    g_best, g_sel = None, None
    for g in range(N_EXPERT_GROUPS):
        rows = b[g * per_group:(g + 1) * per_group]
        top2 = None
        for i in range(per_group):
            for j in range(i + 1, per_group):
                pair = rows[i] + rows[j]
                top2 = pair if top2 is None else jnp.maximum(top2, pair)
        if g == 0:
            g_best, g_sel = top2, jnp.zeros(top2.shape, jnp.int32)
        else:
            upd = top2 > g_best
            g_best = jnp.where(upd, top2, g_best)
            g_sel = jnp.where(upd, g, g_sel)
    neg = jnp.full(g_best.shape, -jnp.inf, F32)
    m = [jnp.where(g_sel == e // per_group, b[e], neg) for e in range(N_EXPERTS)]

    def first_argmax(vals):
        best, idx = vals[0], jnp.zeros(vals[0].shape, jnp.int32)
        for e in range(1, N_EXPERTS):
            upd = vals[e] > best
            best = jnp.where(upd, vals[e], best)
            idx = jnp.where(upd, e, idx)
        return idx

    e1 = first_argmax(m)
    e2 = first_argmax([jnp.where(e1 == e, neg, m[e]) for e in range(N_EXPERTS)])
    zero = jnp.zeros(g_best.shape, F32)
    s1 = sum(jnp.where(e1 == e, s[e], zero) for e in range(N_EXPERTS))
    s2 = sum(jnp.where(e2 == e, s[e], zero) for e in range(N_EXPERTS))
    den = s1 + s2
    hit1 = jnp.concatenate([jnp.where(e1 == e, 1.0, 0.0) for e in range(N_EXPERTS)], axis=0)
    hit2 = jnp.concatenate([jnp.where(e2 == e, 1.0, 0.0) for e in range(N_EXPERTS)], axis=0)
    hits = hit1 + hit2
    tr = lax.broadcasted_iota(jnp.int32, (tt, tt), 0)
    tc = lax.broadcasted_iota(jnp.int32, (tt, tt), 1)
    earlier = jnp.where(tr < tc, 1.0, 0.0).astype(BF16)
    before = cnt_ref[:, 0:1] + jnp.dot(hits.astype(BF16), earlier, preferred_element_type=F32)
    e_ref[0:1, :] = e1
    e_ref[1:2, :] = e2
    gate_ref[0:1, :] = s1 / den
    gate_ref[1:2, :] = s2 / den
    rank_ref[0:1, :] = jnp.sum(hit1 * before, axis=0, keepdims=True).astype(jnp.int32)
    rank_ref[1:2, :] = jnp.sum(hit2 * before, axis=0, keepdims=True).astype(jnp.int32)
    cnt_ref[...] = cnt_ref[...] + jnp.sum(hits, axis=1, keepdims=True)


def _router(f2, wr_t, bias):
    n, d = f2.shape
    tt = int(np.gcd(n, 512))
    row2 =lambda: pl.BlockSpec((2, tt), lambda i: (0, i))
    return pl.pallas_call(
        _router_kernel,
        grid=(n // tt,),
        in_specs=[pl.BlockSpec((tt, d), lambda i: (i, 0)),
                  pl.BlockSpec((N_EXPERTS, d), lambda i: (0, 0)),
                  pl.BlockSpec((N_EXPERTS, 1), lambda i: (0, 0))],
        out_specs=[row2(), row2(), row2(), pl.BlockSpec((N_EXPERTS, LANES), lambda i: (0, 0))],
        out_shape=[jax.ShapeDtypeStruct((2, n), jnp.int32), jax.ShapeDtypeStruct((2, n), F32),
                   jax.ShapeDtypeStruct((2, n), jnp.int32), jax.ShapeDtypeStruct((N_EXPERTS, LANES), F32)],
        compiler_params=_cparams("arbitrary"),
        name="router",
    )(f2, wr_t, bias)


MOE_BLOCK = 512


def _dispatch_kernel(slot_ref, f_hbm, init_hbm, xs_hbm, sem):
    del init_hbm
    td = slot_ref.shape[1]
    base = pl.program_id(0) * td

    def row_copy(r, k):
        return pltpu.make_async_copy(f_hbm.at[pl.ds(base + r, 1)], xs_hbm.at[pl.ds(slot_ref[k, r], 1)], sem)

    def start(r, carry):
        row_copy(r, 0).start()
        row_copy(r, 1).start()
        return carry

    def wait(r, carry):
        row_copy(r, 0).wait()
        row_copy(r, 1).wait()
        return carry

    lax.fori_loop(0, td, start, 0)
    lax.fori_loop(0, td, wait, 0)


def _dispatch(f2, slot, n_slots):
    n, d = f2.shape
    td = min(256, n)
    return pl.pallas_call(
        _dispatch_kernel,
        grid=(n // td,),
        in_specs=[pl.BlockSpec((2, td), lambda i: (0, i), memory_space=pltpu.SMEM),
                  pl.BlockSpec(memory_space=pl.ANY),
                  pl.BlockSpec(memory_space=pl.ANY)],
        out_specs=pl.BlockSpec(memory_space=pl.ANY),
        out_shape=jax.ShapeDtypeStruct((n_slots, d), F32),
        scratch_shapes=[pltpu.SemaphoreType.DMA(())],
        input_output_aliases={2: 0},
        compiler_params=_cparams("arbitrary"),
        name="moe_dispatch",
    )(slot, f2, jnp.zeros((n_slots, d), F32))


def _ffn_kernel(blk_e_ref, n_used_ref, x_ref, wg_ref, wu_ref, wd_ref, y_ref):
    del blk_e_ref

    @pl.when(pl.program_id(0) < n_used_ref[0])
    def _():
        x = x_ref[...].astype(BF16)
        hg = jnp.dot(x, wg_ref[...], preferred_element_type=F32)
        hu = jnp.dot(x, wu_ref[...], preferred_element_type=F32)
        act = (hg * _sigmoid(hg) * hu).astype(BF16)
        y_ref[...] = jnp.dot(act, wd_ref[...], preferred_element_type=F32)

    @pl.when(pl.program_id(0) >= n_used_ref[0])
    def _():
        y_ref[...] = jnp.zeros(y_ref.shape, F32)


def _expert_ffn(xs, blk_expert, n_used, wg, wu, wd):
    n_slots, d = xs.shape
    ff = wg.shape[2]
    bm = MOE_BLOCK
    last = lambda j, nu: jnp.minimum(j, nu[0] - 1)
    return pl.pallas_call(
        _ffn_kernel,
        grid_spec=pltpu.PrefetchScalarGridSpec(
            num_scalar_prefetch=2,
            grid=(n_slots // bm,),
            in_specs=[pl.BlockSpec((bm, d), lambda j, be, nu: (last(j, nu), 0)),
                      pl.BlockSpec((None, d, ff), lambda j, be, nu: (be[last(j, nu)], 0, 0)),
                      pl.BlockSpec((None, d, ff), lambda j, be, nu: (be[last(j, nu)], 0, 0)),
                      pl.BlockSpec((None, ff, d), lambda j, be, nu: (be[last(j, nu)], 0, 0))],
            out_specs=pl.BlockSpec((bm, d), lambda j, be, nu: (j, 0)),
        ),
        out_shape=jax.ShapeDtypeStruct((n_slots, d), F32),
        compiler_params=_cparams("arbitrary"),
        name="expert_ffn",
    )(blk_expert, n_used, xs, wg, wu, wd)


def _combine_kernel(slot_ref, ys_hbm, h_ref, gate_ref, mod_ref, o_ref, buf, sem):
    tc = h_ref.shape[0]

    def row_copy(r, k):
        return pltpu.make_async_copy(ys_hbm.at[pl.ds(slot_ref[k, r], 1)], buf.at[k, pl.ds(r, 1)], sem)

    def start(r, carry):
        row_copy(r, 0).start()
        row_copy(r, 1).start()
        return carry

    def wait(r, carry):
        row_copy(r, 0).wait()
        row_copy(r, 1).wait()
        return carry

    lax.fori_loop(0, tc, start, 0)
    lax.fori_loop(0, tc, wait, 0)
    y = gate_ref[:, 0:1] * buf[0] + gate_ref[:, 1:2] * buf[1]
    o_ref[...] = h_ref[...] + mod_ref[5:6, :] * y


def _combine(ys, slot, gates_t, h, modsel, n_ctx):
    b, t, d = h.shape
    tc = min(128, n_ctx)
    nct = n_ctx // tc
    per_b = t // tc
    return pl.pallas_call(
        _combine_kernel,
        grid=(b, per_b),
        in_specs=[pl.BlockSpec((2, tc), lambda bi, i: (0, bi * per_b + i), memory_space=pltpu.SMEM),
                  pl.BlockSpec(memory_space=pl.ANY),
                  pl.BlockSpec((None, tc, d), lambda bi, i: (bi, i, 0)),
                  pl.BlockSpec((tc, 2), lambda bi, i: (bi * per_b + i, 0)),
                  pl.BlockSpec((None, None, N_MOD, d), lambda bi, i: (bi, jnp.where(i >= nct, 1, 0), 0, 0))],
        out_specs=pl.BlockSpec((None, tc, d), lambda bi, i: (bi, i, 0)),
        out_shape=jax.ShapeDtypeStruct((b, t, d), F32),
        scratch_shapes=[pltpu.VMEM((2, tc, d), F32), pltpu.SemaphoreType.DMA(())],
        compiler_params=_cparams("arbitrary", "arbitrary"),
        name="moe_combine",
    )(slot, ys, h, gates_t, modsel)


def _moe(f, h, modsel, wr_t, bias, wg, wu, wd, n_ctx):
    b, t, d = f.shape
    n = b * t
    f2 = f.reshape(n, d)
    expert, gates, rank, cnt = _router(f2, wr_t, bias)
    bm = MOE_BLOCK
    counts = cnt[:, 0].astype(jnp.int32)
    padded = (counts + bm - 1) // bm * bm
    pend = jnp.cumsum(padded)
    pstart = pend - padded
    slot = pstart[expert] + rank
    n_slots = -(-2 * n // bm) * bm + N_EXPERTS * bm
    n_blocks = n_slots // bm
    blk_expert = jnp.minimum(jnp.searchsorted(pend, jnp.arange(n_blocks, dtype=jnp.int32) * bm, side='right'),
                             N_EXPERTS - 1).astype(jnp.int32)
    n_used = (pend[-1:] // bm).astype(jnp.int32)
    xs = _dispatch(f2, slot, n_slots)
    ys = _expert_ffn(xs, blk_expert, n_used, wg, wu, wd)
    return _combine(ys, slot, gates.T, h, modsel, n_ctx)


def _final_norm_kernel(h_ref, g_ref, o_ref):
    x = h_ref[...]
    o_ref[...] = x * lax.rsqrt(jnp.mean(x * x, axis=-1, keepdims=True) + NORM_EPS) * g_ref[...]


def _final_norm(h, g, n_ctx):
    b, t, d = h.shape
    tm = min(256, n_ctx)
    skip = n_ctx // tm
    return pl.pallas_call(
        _final_norm_kernel,
        grid=(b, (t - n_ctx) // tm),
        in_specs=[pl.BlockSpec((None, tm, d), lambda bi, i: (bi, i + skip, 0)),
                  pl.BlockSpec((1, d), lambda bi, i: (0, 0))],
        out_specs=pl.BlockSpec((None, tm, d), lambda bi, i: (bi, i, 0)),
        out_shape=jax.ShapeDtypeStruct((b, t - n_ctx, d), F32),
        compiler_params=_cparams("parallel", "parallel"),
        name="final_norm",
    )(h, g.reshape(1, d))


def kernel(x, c, ctx, c_ctx, w_mod, b_mod, norm_mix_g, norm_ffn_g, w_in, q_norm_g, k_norm_g, sink_logit,
           rwkv_conv, rwkv_w0, rwkv_w2, rwkv_a0, rwkv_a2, rwkv_g2, rwkv_k_k, rwkv_k_a, rwkv_r_k, rwkv_ln_w,
           rwkv_ln_b, w_out, w_router, router_bias, e_gate, e_up, e_down, final_norm_g):
    b, seq, d = x.shape
    n_ctx = ctx.shape[1]
    depth = w_mod.shape[0]
    cw = RWKV_WIDTH
    h = jnp.concatenate([ctx, x], axis=1)
    cond_rows = -(-(b + 1) // 8) * 8
    cond = jnp.zeros((cond_rows, d), F32).at[:b].set(c).at[b].set(c_ctx)
    cos_t, sin_t = _rope_tables(n_ctx, seq)
    grp = jnp.asarray(np.kron(np.eye(LANES // HEAD_DIM), np.ones((HEAD_DIM, HEAD_DIM))), BF16)
    wr_t = w_router.T
    bias = router_bias.reshape(N_EXPERTS, 1)
    q_order = jnp.asarray(_Q_HEAD_ORDER)
    for l in range(depth):
        mods = _modulation(cond, w_mod[l], b_mod[l])
        modsel = jnp.stack([jnp.broadcast_to(mods[b], (b, N_MOD * d)), mods[:b]], axis=1).reshape(b, 2, N_MOD, d)
        wa = w_in[l][:, :ATT_COLS][:, _ATT_COL_PERM].astype(BF16)
        wr = w_in[l][:, ATT_COLS:].astype(BF16)
        ua, ur = _proj_in(h, modsel, norm_mix_g[l], wa, wr, n_ctx)
        qg = jnp.tile(q_norm_g[l][_HEAD_PERM], LANES // HEAD_DIM).reshape(1, LANES)
        kg = jnp.tile(k_norm_g[l][_HEAD_PERM], LANES // HEAD_DIM).reshape(1, LANES)
        qkv = _attn_prep(ua, cos_t, sin_t, qg, kg, grp, n_ctx)
        o_a = _global_attn(qkv, n_ctx)
        o_w = _window_attn(qkv, sink_logit[l][q_order], n_ctx)
        r_, v_, al_, lw_, kd_, be_, bonus_, gate_ = _rwkv_prep(
            ur, rwkv_conv[l], rwkv_w0[l].reshape(1, 2 * cw), _block_diag2(rwkv_w2[l]),
            rwkv_a0[l].reshape(1, 2 * cw), _block_diag2(rwkv_a2[l]), rwkv_g2[l],
            rwkv_k_k[l].reshape(1, cw), rwkv_k_a[l].reshape(1, cw), rwkv_r_k[l].reshape(2, cw), grp, n_ctx)
        o2 = _rwkv_scan(r_, v_, al_, lw_, kd_, be_, n_ctx)
        o_r = _rwkv_out(o2, bonus_, gate_, rwkv_ln_w[l].reshape(1, cw), rwkv_ln_b[l].reshape(1, cw), grp, n_ctx)
        h, f = _proj_out(h, o_a, o_w, o_r, w_out[l][_OUT_ROW_PERM].astype(BF16), modsel, norm_ffn_g[l], n_ctx)
        h = _moe(f, h, modsel, wr_t, bias, e_gate[l].astype(BF16), e_up[l].astype(BF16), e_down[l].astype(BF16), n_ctx)
    return _final_norm(h, final_norm_g, n_ctx)
```

```python
import functools

import numpy as np
import jax
import jax.numpy as jnp
from jax import lax
from jax.experimental import pallas as pl
from jax.experimental.pallas import tpu as pltpu

D_MODEL = 1024
GRID_W = 64
HEAD_DIM = 64
ATT_HEADS = 4
ATT_KV_HEADS = 2
RWKV_HEADS = 8
RWKV_WIDTH = RWKV_HEADS * HEAD_DIM
WINDOW = 128
Q_BLOCK = 128
ROPE_THETA = 10000.0
DECAY_LORA = 64
ICLR_LORA = 64
GATE_LORA = 128
N_EXPERTS = 16
N_EXPERT_GROUPS = 4
N_MOD = 6
NORM_EPS = 1e-6
GN_EPS = 64e-5
ATT_COLS = 1024
RWKV_COLS = 3 * RWKV_WIDTH + 2 * DECAY_LORA + 2 * ICLR_LORA + GATE_LORA

LANES = 128
VMEM_LIMIT = 56 * 1024 * 1024

F32 = jnp.float32
BF16 = jnp.bfloat16


def _cparams(*sem):
    return pltpu.CompilerParams(dimension_semantics=sem, vmem_limit_bytes=VMEM_LIMIT)


def _dot(a, b):
    return jnp.dot(a.astype(BF16), b.astype(BF16), preferred_element_type=F32)


def _dot_nt(a, b):
    return lax.dot_general(a.astype(BF16), b.astype(BF16), (((1,), (1,)), ((), ())),
                           preferred_element_type=F32)


def _dot_tn(a, b):
    return lax.dot_general(a.astype(BF16), b.astype(BF16), (((0,), (0,)), ((), ())),
                           preferred_element_type=F32)


def _split_hi_lo(a):
    hi = a.astype(BF16)
    lo = (a - hi.astype(F32)).astype(BF16)
    return hi, lo


def _dot_x3(a, b):
    ah, al = _split_hi_lo(a)
    bh, bl = _split_hi_lo(b)
    return (jnp.dot(ah, bh, preferred_element_type=F32) + jnp.dot(al, bh, preferred_element_type=F32)
            + jnp.dot(ah, bl, preferred_element_type=F32))


def _dot_exact_rhs(a, b_bf16):
    ah, al = _split_hi_lo(a)
    return jnp.dot(ah, b_bf16, preferred_element_type=F32) + jnp.dot(al, b_bf16, preferred_element_type=F32)


def _sigmoid(x):
    return 1.0 / (1.0 + jnp.exp(-x))


def _mod_kernel(c_ref, w_ref, b_ref, o_ref):
    c = c_ref[...]
    s = c * _sigmoid(c)
    o_ref[...] = _dot_x3(s, w_ref[...]) + b_ref[...]


def _modulation(cond, w_mod_l, b_mod_l):
    r, d = cond.shape
    n = w_mod_l.shape[1]
    tn = 1024
    return pl.pallas_call(
        _mod_kernel,
        grid=(n // tn,),
        in_specs=[pl.BlockSpec((r, d), lambda j: (0, 0)),
                  pl.BlockSpec((d, tn), lambda j: (0, j)),
                  pl.BlockSpec((1, tn), lambda j: (0, j))],
        out_specs=pl.BlockSpec((r, tn), lambda j: (0, j)),
        out_shape=jax.ShapeDtypeStruct((r, n), F32),
        compiler_params=_cparams("arbitrary"),
        name="modulation",
    )(cond, w_mod_l, b_mod_l.reshape(1, n))


def _proj_in_kernel(h_ref, mod_ref, g_ref, wa_ref, wr_ref, ua_ref, ur_ref):
    x = h_ref[...]
    y = x * lax.rsqrt(jnp.mean(x * x, axis=-1, keepdims=True) + NORM_EPS) * g_ref[...]
    y = (y * (1.0 + mod_ref[1:2, :]) + mod_ref[0:1, :]).astype(BF16)
    ua_ref[...] = jnp.dot(y, wa_ref[...], preferred_element_type=F32)
    ur_ref[...] = jnp.dot(y, wr_ref[...], preferred_element_type=F32)


def _proj_in(h, modsel, g, wa, wr, n_ctx):
    b, t, d = h.shape
    tm = min(256, n_ctx)
    nct = n_ctx // tm
    return pl.pallas_call(
        _proj_in_kernel,
        grid=(b, t // tm),
        in_specs=[pl.BlockSpec((None, tm, d), lambda bi, i: (bi, i, 0)),
                  pl.BlockSpec((None, None, N_MOD, d), lambda bi, i: (bi, jnp.where(i >= nct, 1, 0), 0, 0)),
                  pl.BlockSpec((1, d), lambda bi, i: (0, 0)),
                  pl.BlockSpec(wa.shape, lambda bi, i: (0, 0)),
                  pl.BlockSpec(wr.shape, lambda bi, i: (0, 0))],
        out_specs=[pl.BlockSpec((None, tm, ATT_COLS), lambda bi, i: (bi, i, 0)),
                   pl.BlockSpec((None, tm, RWKV_COLS), lambda bi, i: (bi, i, 0))],
        out_shape=[jax.ShapeDtypeStruct((b, t, ATT_COLS), F32),
                   jax.ShapeDtypeStruct((b, t, RWKV_COLS), F32)],
        compiler_params=_cparams("parallel", "parallel"),
        name="proj_in",
    )(h, modsel, g.reshape(1, d), wa, wr)


def _swap_halves(x):
    lane = lax.broadcasted_iota(jnp.int32, x.shape, 1)
    fwd = pltpu.roll(x, LANES - HEAD_DIM // 2, 1)
    bwd = pltpu.roll(x, HEAD_DIM // 2, 1)
    return jnp.where(lane % HEAD_DIM < HEAD_DIM // 2, fwd, bwd)


def _attn_prep_kernel(ua_ref, cos_ref, sin_ref, qg_ref, kg_ref, grp_ref, o_ref, vt_ref):
    cos, sin = cos_ref[...], sin_ref[...]
    vt_ref[...] = ua_ref[:, 3 * LANES:4 * LANES].T.astype(BF16)
    grp = grp_ref[...]
    scale = HEAD_DIM ** -0.5

    def head_norm(x, g):
        ss = _dot_exact_rhs(x * x, grp) * (1.0 / HEAD_DIM)
        return x * lax.rsqrt(ss + NORM_EPS) * g

    def rope(x):
        return x * cos + _swap_halves(x) * sin

    def blk(c):
        return ua_ref[:, c * LANES:(c + 1) * LANES]

    def put(c, v):
        o_ref[:, c * LANES:(c + 1) * LANES] = v.astype(BF16)

    put(0, rope(head_norm(blk(0), qg_ref[...])) * scale)
    put(1, rope(head_norm(blk(1), qg_ref[...])) * scale)
    put(2, rope(head_norm(blk(2), kg_ref[...])))
    put(3, blk(3))
    put(4, rope(blk(4)) * scale)
    put(5, rope(blk(5)) * scale)
    put(6, rope(blk(6)))
    put(7, blk(7))


def _attn_prep(ua, cos_t, sin_t, qg, kg, grp, n_ctx):
    b, t, _ = ua.shape
    tt = min(256, n_ctx)
    return pl.pallas_call(
        _attn_prep_kernel,
        grid=(b, t // tt),
        in_specs=[pl.BlockSpec((None, tt, ATT_COLS), lambda bi, i: (bi, i, 0)),
                  pl.BlockSpec((tt, LANES), lambda bi, i: (i, 0)),
                  pl.BlockSpec((tt, LANES), lambda bi, i: (i, 0)),
                  pl.BlockSpec((1, LANES), lambda bi, i: (0, 0)),
                  pl.BlockSpec((1, LANES), lambda bi, i: (0, 0)),
                  pl.BlockSpec((LANES, LANES), lambda bi, i: (0, 0))],
        out_specs=[pl.BlockSpec((None, tt, ATT_COLS), lambda bi, i: (bi, i, 0)),
                   pl.BlockSpec((None, None, LANES, tt), lambda bi, i: (bi, i, 0, 0))],
        out_shape=[jax.ShapeDtypeStruct((b, t, ATT_COLS), BF16),
                   jax.ShapeDtypeStruct((b, t // tt, LANES, tt), BF16)],
        compiler_params=_cparams("parallel", "parallel"),
        name="attn_prep",
    )(ua, cos_t, sin_t, qg, kg, grp)


def _global_attn_kernel(q_ref, k_ref, vt_ref, o_ref, m_scr, acc_scr, *, tk, n_ctx_blocks, n_kv_ctx, n_kv_all):
    i = pl.program_id(1)
    tq = q_ref.shape[0]
    row = lax.broadcasted_iota(jnp.int32, (LANES, tq), 0)
    top = row < HEAD_DIM
    row_v = lax.broadcasted_iota(jnp.int32, (LANES, tk), 0) < HEAD_DIM
    qts = []
    for blk in range(2):
        qt = q_ref[:, blk * LANES:(blk + 1) * LANES].astype(F32).T
        qts.append(jnp.where(top, qt, 0.0).astype(BF16))
        qts.append(jnp.where(top, 0.0, qt).astype(BF16))
    m_scr[...] = jnp.full(m_scr.shape, -jnp.inf, F32)
    acc_scr[...] = jnp.zeros(acc_scr.shape, F32)
    one = jnp.ones((), BF16)

    def attend(chunks):
        kcs = [k_ref[pl.ds(pl.multiple_of(c * tk, tk), tk), :] for c in chunks]
        vts = [(jnp.where(row_v, vt_ref[c], one), jnp.where(row_v, one, vt_ref[c])) for c in chunks]
        sts = [[jnp.dot(kc, qts[j], preferred_element_type=F32) for kc in kcs] for j in range(4)]
        ps, alphas = [], []
        for j in range(4):
            m_old = m_scr[j]
            m_new = m_old
            for st in sts[j]:
                m_new = jnp.maximum(m_new, jnp.max(st, axis=0, keepdims=True))
            ps.append([jnp.exp(st - m_new).astype(BF16) for st in sts[j]])
            alphas.append(jnp.exp(m_old - m_new))
            m_scr[j] = m_new
        pvs = [sum(jnp.dot(vts[n][j % 2], ps[j][n], preferred_element_type=F32) for n in range(len(chunks)))
               for j in range(4)]
        for j in range(4):
            acc_scr[j] = alphas[j] * acc_scr[j] + pvs[j]

    for c in range(n_kv_ctx):
        attend([c])
    group = 2 if (n_kv_all - n_kv_ctx) % 2 == 0 else 1
    n_groups = jnp.where(i < n_ctx_blocks, 0, (n_kv_all - n_kv_ctx) // group)

    def body(g, carry):
        attend([n_kv_ctx + g * group + n for n in range(group)])
        return carry

    lax.fori_loop(0, n_groups, body, 0)
    for blk in range(2):
        a0, a1 = acc_scr[2 * blk], acc_scr[2 * blk + 1]
        ot = jnp.where(top, a0 / a0[HEAD_DIM:HEAD_DIM + 1, :], a1 / a1[0:1, :])
        o_ref[:, blk * LANES:(blk + 1) * LANES] = ot.T.astype(BF16)


def _global_attn(qkv, vt, n_ctx):
    b, t, _ = qkv.shape
    tq = min(256, n_ctx)
    tk = vt.shape[3]
    kern = functools.partial(_global_attn_kernel, tk=tk, n_ctx_blocks=n_ctx // tq,
                             n_kv_ctx=n_ctx // tk, n_kv_all=t // tk)
    return pl.pallas_call(
        kern,
        grid=(b, t // tq),
        in_specs=[pl.BlockSpec((None, tq, 2 * LANES), lambda bi, i: (bi, i, 0)),
                  pl.BlockSpec((None, t, LANES), lambda bi, i: (bi, 0, 2)),
                  pl.BlockSpec((None, t // tk, LANES, tk), lambda bi, i: (bi, 0, 0, 0))],
        out_specs=pl.BlockSpec((None, tq, 2 * LANES), lambda bi, i: (bi, i, 0)),
        out_shape=jax.ShapeDtypeStruct((b, t, 2 * LANES), BF16),
        scratch_shapes=[pltpu.VMEM((4, 1, tq), F32), pltpu.VMEM((4, LANES, tq), F32)],
        compiler_params=_cparams("parallel", "arbitrary"),
        name="global_attn",
    )(qkv, qkv, vt)


def _window_attn_kernel(sink_ref, q_ref, k_ref, v_ref, o_ref, *, n_ctx, seq):
    i = pl.program_id(1)
    qb = Q_BLOCK
    band = 3 * qb
    n_ctx_blocks = n_ctx // qb
    is_lat = i >= n_ctx_blocks
    n = i - n_ctx_blocks
    start = jnp.clip((n - 1) * qb, 0, seq - band)
    off = pl.multiple_of(start + n_ctx, qb)
    kc, vc = k_ref[0:n_ctx, :], v_ref[0:n_ctx, :]
    kb, vb = k_ref[pl.ds(off, band), :], v_ref[pl.ds(off, band), :]
    qpos = n * qb + lax.broadcasted_iota(jnp.int32, (qb, band), 0)
    kpos = start + lax.broadcasted_iota(jnp.int32, (qb, band), 1)
    near = (jnp.abs(kpos - qpos) <= WINDOW) & is_lat
    lane = lax.broadcasted_iota(jnp.int32, (qb, LANES), 1)
    low = lane < HEAD_DIM
    zero = jnp.zeros((), BF16)
    outs = []
    for j in range(4):
        blk = q_ref[:, (j // 2) * LANES:(j // 2 + 1) * LANES]
        q = jnp.where(low if j % 2 == 0 else ~low, blk, zero)
        sink = sink_ref[j]
        s_c = _dot_nt(q, kc)
        s_b = jnp.where(near, _dot_nt(q, kb), -jnp.inf)
        m = jnp.maximum(jnp.maximum(jnp.max(s_c, axis=-1, keepdims=True), jnp.max(s_b, axis=-1, keepdims=True)), sink)
        p_c = jnp.exp(s_c - m)
        p_b = jnp.exp(s_b - m)
        den = jnp.sum(p_c, axis=-1, keepdims=True) + jnp.sum(p_b, axis=-1, keepdims=True) + jnp.exp(sink - m)
        acc = jnp.dot(p_c.astype(BF16), vc, preferred_element_type=F32) + jnp.dot(p_b.astype(BF16), vb, preferred_element_type=F32)
        outs.append(acc / den)
    for blk in range(2):
        o_ref[:, blk * LANES:(blk + 1) * LANES] = jnp.where(low, outs[2 * blk], outs[2 * blk + 1]).astype(BF16)


def _window_attn(qkv, sink, n_ctx):
    b, t, _ = qkv.shape
    kern = functools.partial(_window_attn_kernel, n_ctx=n_ctx, seq=t - n_ctx)
    return pl.pallas_call(
        kern,
        grid_spec=pltpu.PrefetchScalarGridSpec(
            num_scalar_prefetch=1,
            grid=(b, t // Q_BLOCK),
            in_specs=[pl.BlockSpec((None, Q_BLOCK, 2 * LANES), lambda bi, i, s: (bi, i, 2)),
                      pl.BlockSpec((None, t, LANES), lambda bi, i, s: (bi, 0, 6)),
                      pl.BlockSpec((None, t, LANES), lambda bi, i, s: (bi, 0, 7))],
            out_specs=pl.BlockSpec((None, Q_BLOCK, 2 * LANES), lambda bi, i, s: (bi, i, 0)),
        ),
        out_shape=jax.ShapeDtypeStruct((b, t, 2 * LANES), BF16),
        compiler_params=_cparams("parallel", "arbitrary"),
        name="window_attn",
    )(sink, qkv, qkv, qkv)


_HEAD_PERM = np.concatenate([np.arange(0, HEAD_DIM, 2), np.arange(1, HEAD_DIM, 2)])
_Q_HEAD_ORDER = (0, 2, 1, 3)


def _q_cols(base, rotary):
    inner = _HEAD_PERM if rotary else np.arange(HEAD_DIM)
    return np.concatenate([base + h * HEAD_DIM + inner for h in _Q_HEAD_ORDER])


def _kv_cols(base, rotary):
    inner = _HEAD_PERM if rotary else np.arange(HEAD_DIM)
    return np.concatenate([base + h * HEAD_DIM + inner for h in range(ATT_KV_HEADS)])


_ATT_COL_PERM = np.concatenate([
    _q_cols(0, True), _kv_cols(256, True), _kv_cols(384, False),
    _q_cols(512, True), _kv_cols(768, True), _kv_cols(896, False)])
_OUT_ROW_PERM = np.concatenate([_q_cols(0, False), _q_cols(256, False), np.arange(512, 1024)])


def _rope_tables(n_ctx, seq):
    rows = seq // GRID_W
    row = jnp.repeat(jnp.arange(rows, dtype=F32), GRID_W)
    col = jnp.tile(jnp.arange(GRID_W, dtype=F32), rows)
    n_freq = HEAD_DIM // 4
    inv = ROPE_THETA ** (-jnp.arange(n_freq, dtype=F32) / n_freq)
    ang = jnp.concatenate([row[:, None] * inv, col[:, None] * inv], axis=-1)
    cos = jnp.concatenate([jnp.ones((n_ctx, HEAD_DIM // 2), F32), jnp.cos(ang)], axis=0)
    sin = jnp.concatenate([jnp.zeros((n_ctx, HEAD_DIM // 2), F32), jnp.sin(ang)], axis=0)
    cos_t = jnp.tile(jnp.concatenate([cos, cos], axis=-1), (1, LANES // HEAD_DIM))
    sin_t = jnp.tile(jnp.concatenate([-sin, sin], axis=-1), (1, LANES // HEAD_DIM))
    return cos_t, sin_t


def _rwkv_prep_kernel(cur_ref, prev_ref, next_ref, conv_ref, w0_ref, w2_ref, a0_ref, a2_ref, g2_ref,
                      kk_ref, ka_ref, rk_ref, grp_ref,
                      r_out, v_out, al_out, lw_out, kd_out, be_out, bonus_out, gate_out, *, n_ctx_tiles, n_tiles):
    i = pl.program_id(1)
    tt = cur_ref.shape[0]
    c = RWKV_WIDTH
    x = cur_ref[...]
    row = lax.broadcasted_iota(jnp.int32, (tt, 1), 0)
    has_prev = jnp.logical_and(i != 0, i != n_ctx_tiles)
    has_next = jnp.logical_and(i != n_ctx_tiles - 1, i != n_tiles - 1)
    halo_prev = jnp.where(has_prev, prev_ref[7:8, :], 0.0)
    halo_next = jnp.where(has_next, next_ref[0:1, :], 0.0)
    x_prev = jnp.where(row == 0, halo_prev, pltpu.roll(x, 1, 0))
    x_next = jnp.where(row == tt - 1, halo_next, pltpu.roll(x, tt - 1, 0))
    t = x_prev * conv_ref[0:1, :] + x * conv_ref[1:2, :] + x_next * conv_ref[2:3, :]
    r, k, v = t[:, 0:c], t[:, c:2 * c], t[:, 2 * c:3 * c]
    wd = t[:, 3 * c:3 * c + 2 * DECAY_LORA]
    ad = t[:, 3 * c + 2 * DECAY_LORA:3 * c + 2 * DECAY_LORA + 2 * ICLR_LORA]
    gd = t[:, 3 * c + 2 * DECAY_LORA + 2 * ICLR_LORA:]
    z = w0_ref[...] + _dot_x3(jnp.tanh(wd), w2_ref[...])
    neg_softplus = -(jnp.maximum(-z, 0.0) + jnp.log(1.0 + jnp.exp(-jnp.abs(z))))
    lw_out[...] = -jnp.exp(neg_softplus - 0.5)
    a = _sigmoid(a0_ref[...] + _dot_x3(ad, a2_ref[...]))
    grp = grp_ref[...]
    kk = k * kk_ref[...]
    ss = jnp.concatenate([_dot_exact_rhs(kk[:, j * LANES:(j + 1) * LANES] * kk[:, j * LANES:(j + 1) * LANES], grp)
                          for j in range(c // LANES)], axis=-1)
    kk = kk / jnp.maximum(jnp.sqrt(ss), 1e-12)
    bonus_dot = jnp.zeros((tt, c), F32)
    for j in range(2):
        a_j = a[:, j * c:(j + 1) * c]
        kd_j = k * (1.0 + (a_j - 1.0) * ka_ref[...])
        kd_out[:, j * c:(j + 1) * c] = kd_j
        be_out[:, j * c:(j + 1) * c] = a_j * kk
        bonus_dot = bonus_dot + r * kd_j * rk_ref[j:j + 1, :]
    bsum = jnp.concatenate([_dot_exact_rhs(bonus_dot[:, j * LANES:(j + 1) * LANES], grp)
                            for j in range(c // LANES)], axis=-1)
    r_out[...] = r
    v_out[...] = v
    al_out[...] = kk
    bonus_out[...] = bsum * v
    gate_out[...] = _dot_x3(_sigmoid(gd), g2_ref[...])


def _rwkv_prep(ur, conv, w0, w2blk, a0, a2blk, g2, k_k, k_a, r_k, grp, n_ctx):
    b, t, cols = ur.shape
    c = RWKV_WIDTH
    tt = min(256, n_ctx)
    n_tiles = t // tt
    per8 = tt // 8
    kern = functools.partial(_rwkv_prep_kernel, n_ctx_tiles=n_ctx // tt, n_tiles=n_tiles)
    const = lambda shape: pl.BlockSpec(shape, lambda bi, i: (0,) * len(shape))
    tok = lambda w: pl.BlockSpec((None, tt, w), lambda bi, i: (bi, i, 0))
    return pl.pallas_call(
        kern,
        grid=(b, n_tiles),
        in_specs=[tok(cols),
                  pl.BlockSpec((None, 8, cols), lambda bi, i: (bi, jnp.maximum(i * per8 - 1, 0), 0)),
                  pl.BlockSpec((None, 8, cols), lambda bi, i: (bi, jnp.minimum((i + 1) * per8, t // 8 - 1), 0)),
                  const((3, cols)), const((1, 2 * c)), const((2 * DECAY_LORA, 2 * c)), const((1, 2 * c)),
                  const((2 * ICLR_LORA, 2 * c)), const((GATE_LORA, c)), const((1, c)), const((1, c)),
                  const((2, c)), const((LANES, LANES))],
        out_specs=[tok(c), tok(c), tok(c), tok(2 * c), tok(2 * c), tok(2 * c), tok(c), tok(c)],
        out_shape=[jax.ShapeDtypeStruct((b, t, w), F32) for w in (c, c, c, 2 * c, 2 * c, 2 * c, c, c)],
        compiler_params=_cparams("parallel", "parallel"),
        name="rwkv_prep",
    )(ur, ur, ur, conv, w0, w2blk, a0, a2blk, g2, k_k, k_a, r_k, grp)


RWKV_CHUNK = 64
INV_BLOCK = 16


def _rwkv_masks():
    n2 = 2 * RWKV_CHUNK
    rr, cc = np.indices((n2, n2))
    same = (rr // RWKV_CHUNK) == (cc // RWKV_CHUNK)
    masks = np.stack([same & (cc < rr), same & (cc <= rr), same & (cc > rr), same & (cc >= rr),
                      (rr // INV_BLOCK) == (cc // INV_BLOCK), rr == cc]).astype(np.float32)
    tr, tc = np.indices((RWKV_CHUNK, RWKV_CHUNK))
    tri = np.stack([tc <= tr, tc >= tr]).astype(np.float32)
    return jnp.asarray(masks), jnp.asarray(tri, BF16)


def _rwkv_scan_kernel(rf_ref, vf_ref, alf_ref, lwf_ref, kdf_ref, bef_ref,
                      rb_ref, vb_ref, alb_ref, lwb_ref, kdb_ref, beb_ref,
                      mask_ref, tri_ref, of_ref, ob_ref, s_scr):
    cs = RWKV_CHUNK

    @pl.when(pl.program_id(1) == 0)
    def _():
        s_scr[...] = jnp.zeros(s_scr.shape, F32)

    blk16, eye = mask_ref[4], mask_ref[5]
    low = lax.broadcasted_iota(jnp.int32, (cs, LANES), 1) < HEAD_DIM
    bf = lambda x: x.astype(BF16)
    mm = lambda a, b: jnp.dot(a, b, preferred_element_type=F32)
    nt = lambda a, b: lax.dot_general(a, b, (((1,), (1,)), ((), ())), preferred_element_type=F32)
    tn = lambda a, b: lax.dot_general(a, b, (((0,), (0,)), ((), ())), preferred_element_type=F32)

    def stack(x):
        return jnp.concatenate([jnp.where(low, x, 0.0), jnp.where(low, 0.0, x)], axis=0)

    chains = []
    for d, (r_ref, v_ref, al_ref, lw_ref, kd_ref, be_ref, o_ref) in enumerate((
            (rf_ref, vf_ref, alf_ref, lwf_ref, kdf_ref, bef_ref, of_ref),
            (rb_ref, vb_ref, alb_ref, lwb_ref, kdb_ref, beb_ref, ob_ref))):
        lw = lw_ref[...]
        lw_hi, lw_lo = _split_hi_lo(lw)
        tri = tri_ref[d]
        cl = mm(tri, lw_hi) + mm(tri, lw_lo)
        tot = jnp.sum(lw, axis=0, keepdims=True)
        e_neg, e_end = jnp.exp(-cl), jnp.exp(tot - cl)
        a_t = al_ref[...] * jnp.exp(cl - lw)
        r_t = r_ref[...] * jnp.exp(cl)
        k_t, b_t = kd_ref[...] * e_neg, be_ref[...] * e_neg
        k_e, b_e = kd_ref[...] * e_end, be_ref[...] * e_end
        p_end = jnp.exp(tot)
        v = v_ref[...]
        for p in range(RWKV_HEADS // 2):
            sl = slice(p * LANES, (p + 1) * LANES)
            xa, xr = stack(a_t[:, sl]), stack(r_t[:, sl])
            chains.append(dict(
                d=d, p=p, sl=sl, o_ref=o_ref, strict=mask_ref[2 * d], incl=mask_ref[2 * d + 1],
                xa=xa, xr=xr, lhs=bf(jnp.concatenate([xa, xr], axis=0)),
                yb=bf(jnp.concatenate([b_t[:, sl], b_t[:, sl]], axis=0)),
                yk=bf(jnp.concatenate([k_t[:, sl], k_t[:, sl]], axis=0)),
                vs=bf(stack(v[:, sl])), ks=bf(stack(k_e[:, sl])), bs=bf(stack(b_e[:, sl])), pe=p_end[:, sl]))
    n2 = 2 * cs
    for c in chains:
        c["xb"] = nt(c["lhs"], c["yb"])
        c["xk"] = nt(c["lhs"], c["yk"])
    for c in chains:
        l_m = c["xb"][:n2] * c["strict"]
        c["m_rb"] = bf(c["xb"][n2:] * c["incl"])
        c["m_ak"] = bf(c["xk"][:n2] * c["strict"])
        c["m_rk"] = bf(c["xk"][n2:] * c["incl"])
        l_d = l_m * blk16
        c["l_d"], c["l_o"] = l_d, bf(l_m - l_d)
    for c in chains:
        ld = bf(c["l_d"])
        c["l2"] = mm(ld, ld)
        c["mv"] = mm(c["m_ak"], c["vs"])
        c["rkv"] = mm(c["m_rk"], c["vs"])
        c["ktv"] = tn(c["ks"], c["vs"])
    for c in chains:
        l2 = bf(c["l2"])
        c["l4"] = mm(l2, l2)
        c["p1"] = mm(bf(eye - c["l_d"]), bf(eye + c["l2"]))
    for c in chains:
        l4 = bf(c["l4"])
        c["l8"] = mm(l4, l4)
        c["p2"] = mm(bf(c["p1"]), bf(eye + c["l4"]))
    for c in chains:
        c["dinv"] = bf(mm(bf(c["p2"]), bf(eye + c["l8"])))
    for c in chains:
        c["n"] = bf(mm(c["dinv"], c["l_o"]))
        c["dr"] = mm(c["dinv"], jnp.concatenate([bf(c["xa"]), bf(c["mv"])], axis=1))
    for c in chains:
        c["nsq"] = bf(mm(c["n"], c["n"]))
    for c in chains:
        c["z"] = c["dr"] + mm(c["nsq"], bf(c["dr"]))
    for c in chains:
        c["wu"] = bf(c["z"] - mm(c["n"], bf(c["z"])))
    for c in chains:
        q = mm(c["m_rb"], c["wu"])
        bt = tn(c["bs"], c["wu"])
        rqs = c["xr"] - q[:, :LANES]
        y0s = c["rkv"] - q[:, LANES:]
        c["rq"] = bf(rqs[:cs] + rqs[cs:])
        c["y0"] = y0s[:cs] + y0s[cs:]
        c["m_t"] = bf(eye * c["pe"] - bt[:, :LANES])
        c["g_t"] = c["ktv"] - bt[:, LANES:]
    for c in chains:
        s_old = bf(s_scr[c["d"], c["p"]])
        c["o_ref"][:, c["sl"]] = mm(c["rq"], s_old) + c["y0"]
        s_scr[c["d"], c["p"]] = mm(c["m_t"], s_old) + c["g_t"]


def _rwkv_scan(r, v, al, lw, kd, be, n_ctx):
    b, t, c = r.shape
    cs = RWKV_CHUNK
    ncc, ntot = n_ctx // cs, t // cs
    masks, tri = _rwkv_masks()

    def back(s):
        return jnp.where(s < ncc, ncc - 1 - s, ntot - 1 - s + ncc)

    fwd = lambda col: pl.BlockSpec((None, cs, c), lambda bi, s: (bi, s, col))
    bwd = lambda col: pl.BlockSpec((None, cs, c), lambda bi, s: (bi, back(s), col))
    const = lambda a: pl.BlockSpec(a.shape, lambda bi, s: (0,) * a.ndim)
    return pl.pallas_call(
        _rwkv_scan_kernel,
        grid=(b, ntot),
        in_specs=[fwd(0), fwd(0), fwd(0), fwd(0), fwd(0), fwd(0),
                  bwd(0), bwd(0), bwd(0), bwd(1), bwd(1), bwd(1), const(masks), const(tri)],
        out_specs=[pl.BlockSpec((None, cs, c), lambda bi, s: (bi, s, 0)),
                   pl.BlockSpec((None, cs, c), lambda bi, s: (bi, back(s), 0))],
        out_shape=[jax.ShapeDtypeStruct((b, t, c), F32), jax.ShapeDtypeStruct((b, t, c), F32)],
        scratch_shapes=[pltpu.VMEM((2, RWKV_HEADS // 2, LANES, LANES), F32)],
        compiler_params=_cparams("parallel", "arbitrary"),
        name="rwkv_scan",
    )(r, v, al, lw, kd, be, r, v, al, lw, kd, be, masks, tri)


def _rwkv_out_kernel(of_ref, ob_ref, bonus_ref, gate_ref, lnw_ref, lnb_ref, grp_ref, o_ref):
    grp = grp_ref[...]
    for j in range(RWKV_WIDTH // LANES):
        sl = slice(j * LANES, (j + 1) * LANES)
        o = of_ref[:, sl] + ob_ref[:, sl]
        mu = _dot_exact_rhs(o, grp) * (1.0 / HEAD_DIM)
        d = o - mu
        var = _dot_exact_rhs(d * d, grp) * (1.0 / HEAD_DIM)
        on = d * lax.rsqrt(var + GN_EPS) * lnw_ref[:, sl] + lnb_ref[:, sl]
        o_ref[:, sl] = ((on + bonus_ref[:, sl]) * gate_ref[:, sl]).astype(BF16)


def _rwkv_out(o_f, o_b, bonus, gate, ln_w, ln_b, grp, n_ctx):
    b, t, c = o_f.shape
    tt = min(256, n_ctx)
    tok = pl.BlockSpec((None, tt, c), lambda bi, i: (bi, i, 0))
    const = lambda shape: pl.BlockSpec(shape, lambda bi, i: (0,) * len(shape))
    return pl.pallas_call(
        _rwkv_out_kernel,
        grid=(b, t // tt),
        in_specs=[tok, tok, tok, tok, const((1, c)), const((1, c)), const((LANES, LANES))],
        out_specs=tok,
        out_shape=jax.ShapeDtypeStruct((b, t, c), BF16),
        compiler_params=_cparams("parallel", "parallel"),
        name="rwkv_out",
    )(o_f, o_b, bonus, gate, ln_w, ln_b, grp)


def _block_diag2(w):
    z = jnp.zeros_like(w[0])
    return jnp.concatenate([jnp.concatenate([w[0], z], axis=1), jnp.concatenate([z, w[1]], axis=1)], axis=0)


def _proj_out_kernel(h_ref, oa_ref, ow_ref, or_ref, w_ref, mod_ref, g_ref, hn_ref, f_ref):
    na, nw = oa_ref.shape[1], ow_ref.shape[1]
    mix = (jnp.dot(oa_ref[...], w_ref[0:na, :], preferred_element_type=F32)
           + jnp.dot(ow_ref[...], w_ref[na:na + nw, :], preferred_element_type=F32)
           + jnp.dot(or_ref[...], w_ref[na + nw:, :], preferred_element_type=F32))
    h = h_ref[...] + mod_ref[2:3, :] * mix
    hn_ref[...] = h
    y = h * lax.rsqrt(jnp.mean(h * h, axis=-1, keepdims=True) + NORM_EPS) * g_ref[...]
    f_ref[...] = y * (1.0 + mod_ref[4:5, :]) + mod_ref[3:4, :]


def _proj_out(h, o_a, o_w, o_r, w_out, modsel, g, n_ctx):
    b, t, d = h.shape
    tm = min(256, n_ctx)
    nct = n_ctx // tm
    tok = lambda w: pl.BlockSpec((None, tm, w), lambda bi, i: (bi, i, 0))
    return pl.pallas_call(
        _proj_out_kernel,
        grid=(b, t // tm),
        in_specs=[tok(d), tok(o_a.shape[2]), tok(o_w.shape[2]), tok(o_r.shape[2]),
                  pl.BlockSpec(w_out.shape, lambda bi, i: (0, 0)),
                  pl.BlockSpec((None, None, N_MOD, d), lambda bi, i: (bi, jnp.where(i >= nct, 1, 0), 0, 0)),
                  pl.BlockSpec((1, d), lambda bi, i: (0, 0))],
        out_specs=[tok(d), tok(d)],
        out_shape=[jax.ShapeDtypeStruct((b, t, d), F32), jax.ShapeDtypeStruct((b, t, d), F32)],
        compiler_params=_cparams("parallel", "parallel"),
        name="proj_out",
    )(h, o_a, o_w, o_r, w_out, modsel, g.reshape(1, d))


def _router_kernel(f_ref, wr_ref, bias_ref, e_ref, gate_ref, rank_ref, cnt_ref):
    tt = f_ref.shape[0]
    per_group = N_EXPERTS // N_EXPERT_GROUPS

    @pl.when(pl.program_id(0) == 0)
    def _():
        cnt_ref[...] = jnp.zeros(cnt_ref.shape, F32)

    logits = _dot_nt(wr_ref[...], f_ref[...])
    score = _sigmoid(logits)
    biased = score + bias_ref[...]
    b = [biased[e:e + 1, :] for e in range(N_EXPERTS)]
    s = [score[e:e + 1, :] for e in range(N_EXPERTS)]
    g_best, g_sel = None, None
    for g in range(N_EXPERT_GROUPS):
        rows = b[g * per_group:(g + 1) * per_group]
        top2 = None
        for i in range(per_group):
            for j in range(i + 1, per_group):
                pair = rows[i] + rows[j]
                top2 = pair if top2 is None else jnp.maximum(top2, pair)
        if g == 0:
            g_best, g_sel = top2, jnp.zeros(top2.shape, jnp.int32)
        else:
            upd = top2 > g_best
            g_best = jnp.where(upd, top2, g_best)
            g_sel = jnp.where(upd, g, g_sel)
    neg = jnp.full(g_best.shape, -jnp.inf, F32)
    m = [jnp.where(g_sel == e // per_group, b[e], neg) for e in range(N_EXPERTS)]

    def first_argmax(vals):
        best, idx = vals[0], jnp.zeros(vals[0].shape, jnp.int32)
        for e in range(1, N_EXPERTS):
            upd = vals[e] > best
            best = jnp.where(upd, vals[e], best)
            idx = jnp.where(upd, e, idx)
        return idx

    e1 = first_argmax(m)
    e2 = first_argmax([jnp.where(e1 == e, neg, m[e]) for e in range(N_EXPERTS)])
    zero = jnp.zeros(g_best.shape, F32)
    s1 = sum(jnp.where(e1 == e, s[e], zero) for e in range(N_EXPERTS))
    s2 = sum(jnp.where(e2 == e, s[e], zero) for e in range(N_EXPERTS))
    den = s1 + s2
    hit1 = jnp.concatenate([jnp.where(e1 == e, 1.0, 0.0) for e in range(N_EXPERTS)], axis=0)
    hit2 = jnp.concatenate([jnp.where(e2 == e, 1.0, 0.0) for e in range(N_EXPERTS)], axis=0)
    hits = hit1 + hit2
    tr = lax.broadcasted_iota(jnp.int32, (tt, tt), 0)
    tc = lax.broadcasted_iota(jnp.int32, (tt, tt), 1)
    earlier = jnp.where(tr < tc, 1.0, 0.0).astype(BF16)
    before = cnt_ref[:, 0:1] + jnp.dot(hits.astype(BF16), earlier, preferred_element_type=F32)
    e_ref[0:1, :] = e1
    e_ref[1:2, :] = e2
    gate_ref[0:1, :] = s1 / den
    gate_ref[1:2, :] = s2 / den
    rank_ref[0:1, :] = jnp.sum(hit1 * before, axis=0, keepdims=True).astype(jnp.int32)
    rank_ref[1:2, :] = jnp.sum(hit2 * before, axis=0, keepdims=True).astype(jnp.int32)
    cnt_ref[...] = cnt_ref[...] + jnp.sum(hits, axis=1, keepdims=True)


def _router(f2, wr_t, bias):
    n, d = f2.shape
    tt = int(np.gcd(n, 512))
    row2 =lambda: pl.BlockSpec((2, tt), lambda i: (0, i))
    return pl.pallas_call(
        _router_kernel,
        grid=(n // tt,),
        in_specs=[pl.BlockSpec((tt, d), lambda i: (i, 0)),
                  pl.BlockSpec((N_EXPERTS, d), lambda i: (0, 0)),
                  pl.BlockSpec((N_EXPERTS, 1), lambda i: (0, 0))],
        out_specs=[row2(), row2(), row2(), pl.BlockSpec((N_EXPERTS, LANES), lambda i: (0, 0))],
        out_shape=[jax.ShapeDtypeStruct((2, n), jnp.int32), jax.ShapeDtypeStruct((2, n), F32),
                   jax.ShapeDtypeStruct((2, n), jnp.int32), jax.ShapeDtypeStruct((N_EXPERTS, LANES), F32)],
        compiler_params=_cparams("arbitrary"),
        name="router",
    )(f2, wr_t, bias)


MOE_BLOCK = 512


def _dispatch_kernel(slot_ref, f_ref, init_hbm, xs_hbm, sem):
    del init_hbm
    td = slot_ref.shape[1]

    def row_copy(r, k):
        return pltpu.make_async_copy(f_ref.at[pl.ds(r, 1)], xs_hbm.at[pl.ds(slot_ref[k, r], 1)], sem)

    def start(r, carry):
        row_copy(r, 0).start()
        row_copy(r, 1).start()
        return carry

    def wait(r, carry):
        row_copy(r, 0).wait()
        row_copy(r, 1).wait()
        return carry

    lax.fori_loop(0, td, start, 0)
    lax.fori_loop(0, td, wait, 0)


def _dispatch(f2, slot, n_slots):
    n, d = f2.shape
    td = min(256, n)
    return pl.pallas_call(
        _dispatch_kernel,
        grid=(n // td,),
        in_specs=[pl.BlockSpec((2, td), lambda i: (0, i), memory_space=pltpu.SMEM),
                  pl.BlockSpec((td, d), lambda i: (i, 0)),
                  pl.BlockSpec(memory_space=pl.ANY)],
        out_specs=pl.BlockSpec(memory_space=pl.ANY),
        out_shape=jax.ShapeDtypeStruct((n_slots, d), F32),
        scratch_shapes=[pltpu.SemaphoreType.DMA(())],
        input_output_aliases={2: 0},
        compiler_params=_cparams("arbitrary"),
        name="moe_dispatch",
    )(slot, f2, jnp.zeros((n_slots, d), F32))


def _ffn_kernel(blk_e_ref, n_used_ref, x_ref, wg_ref, wu_ref, wd_ref, y_ref):
    del blk_e_ref

    @pl.when(pl.program_id(0) < n_used_ref[0])
    def _():
        x = x_ref[...].astype(BF16)
        hg = jnp.dot(x, wg_ref[...], preferred_element_type=F32)
        hu = jnp.dot(x, wu_ref[...], preferred_element_type=F32)
        act = (hg * _sigmoid(hg) * hu).astype(BF16)
        y_ref[...] = jnp.dot(act, wd_ref[...], preferred_element_type=F32)

    @pl.when(pl.program_id(0) >= n_used_ref[0])
    def _():
        y_ref[...] = jnp.zeros(y_ref.shape, F32)


def _expert_ffn(xs, blk_expert, n_used, wg, wu, wd):
    n_slots, d = xs.shape
    ff = wg.shape[2]
    bm = MOE_BLOCK
    last = lambda j, nu: jnp.minimum(j, nu[0] - 1)
    return pl.pallas_call(
        _ffn_kernel,
        grid_spec=pltpu.PrefetchScalarGridSpec(
            num_scalar_prefetch=2,
            grid=(n_slots // bm,),
            in_specs=[pl.BlockSpec((bm, d), lambda j, be, nu: (last(j, nu), 0)),
                      pl.BlockSpec((None, d, ff), lambda j, be, nu: (be[last(j, nu)], 0, 0)),
                      pl.BlockSpec((None, d, ff), lambda j, be, nu: (be[last(j, nu)], 0, 0)),
                      pl.BlockSpec((None, ff, d), lambda j, be, nu: (be[last(j, nu)], 0, 0))],
            out_specs=pl.BlockSpec((bm, d), lambda j, be, nu: (j, 0)),
        ),
        out_shape=jax.ShapeDtypeStruct((n_slots, d), F32),
        compiler_params=_cparams("arbitrary"),
        name="expert_ffn",
    )(blk_expert, n_used, xs, wg, wu, wd)


def _combine_kernel(slot_ref, ys_hbm, h_ref, gate_ref, mod_ref, o_ref, buf, sem):
    tc = h_ref.shape[0]

    def row_copy(r, k):
        return pltpu.make_async_copy(ys_hbm.at[pl.ds(slot_ref[k, r], 1)], buf.at[k, pl.ds(r, 1)], sem)

    def start(r, carry):
        row_copy(r, 0).start()
        row_copy(r, 1).start()
        return carry

    def wait(r, carry):
        row_copy(r, 0).wait()
        row_copy(r, 1).wait()
        return carry

    lax.fori_loop(0, tc, start, 0)
    lax.fori_loop(0, tc, wait, 0)
    y = gate_ref[:, 0:1] * buf[0] + gate_ref[:, 1:2] * buf[1]
    o_ref[...] = h_ref[...] + mod_ref[5:6, :] * y


def _combine(ys, slot, gates_t, h, modsel, n_ctx):
    b, t, d = h.shape
    tc = min(128, n_ctx)
    nct = n_ctx // tc
    per_b = t // tc
    return pl.pallas_call(
        _combine_kernel,
        grid=(b, per_b),
        in_specs=[pl.BlockSpec((2, tc), lambda bi, i: (0, bi * per_b + i), memory_space=pltpu.SMEM),
                  pl.BlockSpec(memory_space=pl.ANY),
                  pl.BlockSpec((None, tc, d), lambda bi, i: (bi, i, 0)),
                  pl.BlockSpec((tc, 2), lambda bi, i: (bi * per_b + i, 0)),
                  pl.BlockSpec((None, None, N_MOD, d), lambda bi, i: (bi, jnp.where(i >= nct, 1, 0), 0, 0))],
        out_specs=pl.BlockSpec((None, tc, d), lambda bi, i: (bi, i, 0)),
        out_shape=jax.ShapeDtypeStruct((b, t, d), F32),
        scratch_shapes=[pltpu.VMEM((2, tc, d), F32), pltpu.SemaphoreType.DMA(())],
        compiler_params=_cparams("arbitrary", "arbitrary"),
        name="moe_combine",
    )(slot, ys, h, gates_t, modsel)


def _moe(f, h, modsel, wr_t, bias, wg, wu, wd, n_ctx):
    b, t, d = f.shape
    n = b * t
    f2 = f.reshape(n, d)
    expert, gates, rank, cnt = _router(f2, wr_t, bias)
    bm = MOE_BLOCK
    counts = cnt[:, 0].astype(jnp.int32)
    padded = (counts + bm - 1) // bm * bm
    pend = jnp.cumsum(padded)
    pstart = pend - padded
    slot = pstart[expert] + rank
    n_slots = -(-2 * n // bm) * bm + N_EXPERTS * bm
    n_blocks = n_slots // bm
    blk_expert = jnp.minimum(jnp.searchsorted(pend, jnp.arange(n_blocks, dtype=jnp.int32) * bm, side='right'),
                             N_EXPERTS - 1).astype(jnp.int32)
    n_used = (pend[-1:] // bm).astype(jnp.int32)
    xs = _dispatch(f2, slot, n_slots)
    ys = _expert_ffn(xs, blk_expert, n_used, wg, wu, wd)
    return _combine(ys, slot, gates.T, h, modsel, n_ctx)


def _final_norm_kernel(h_ref, g_ref, o_ref):
    x = h_ref[...]
    o_ref[...] = x * lax.rsqrt(jnp.mean(x * x, axis=-1, keepdims=True) + NORM_EPS) * g_ref[...]


def _final_norm(h, g, n_ctx):
    b, t, d = h.shape
    tm = min(256, n_ctx)
    skip = n_ctx // tm
    return pl.pallas_call(
        _final_norm_kernel,
        grid=(b, (t - n_ctx) // tm),
        in_specs=[pl.BlockSpec((None, tm, d), lambda bi, i: (bi, i + skip, 0)),
                  pl.BlockSpec((1, d), lambda bi, i: (0, 0))],
        out_specs=pl.BlockSpec((None, tm, d), lambda bi, i: (bi, i, 0)),
        out_shape=jax.ShapeDtypeStruct((b, t - n_ctx, d), F32),
        compiler_params=_cparams("parallel", "parallel"),
        name="final_norm",
    )(h, g.reshape(1, d))


def kernel(x, c, ctx, c_ctx, w_mod, b_mod, norm_mix_g, norm_ffn_g, w_in, q_norm_g, k_norm_g, sink_logit,
           rwkv_conv, rwkv_w0, rwkv_w2, rwkv_a0, rwkv_a2, rwkv_g2, rwkv_k_k, rwkv_k_a, rwkv_r_k, rwkv_ln_w,
           rwkv_ln_b, w_out, w_router, router_bias, e_gate, e_up, e_down, final_norm_g):
    b, seq, d = x.shape
    n_ctx = ctx.shape[1]
    depth = w_mod.shape[0]
    cw = RWKV_WIDTH
    h = jnp.concatenate([ctx, x], axis=1)
    cond_rows = -(-(b + 1) // 8) * 8
    cond = jnp.zeros((cond_rows, d), F32).at[:b].set(c).at[b].set(c_ctx)
    cos_t, sin_t = _rope_tables(n_ctx, seq)
    grp = jnp.asarray(np.kron(np.eye(LANES // HEAD_DIM), np.ones((HEAD_DIM, HEAD_DIM))), BF16)
    wr_t = w_router.T
    bias = router_bias.reshape(N_EXPERTS, 1)
    q_order = jnp.asarray(_Q_HEAD_ORDER)
    for l in range(depth):
        mods = _modulation(cond, w_mod[l], b_mod[l])
        modsel = jnp.stack([jnp.broadcast_to(mods[b], (b, N_MOD * d)), mods[:b]], axis=1).reshape(b, 2, N_MOD, d)
        wa = w_in[l][:, :ATT_COLS][:, _ATT_COL_PERM].astype(BF16)
        wr = w_in[l][:, ATT_COLS:].astype(BF16)
        ua, ur = _proj_in(h, modsel, norm_mix_g[l], wa, wr, n_ctx)
        qg = jnp.tile(q_norm_g[l][_HEAD_PERM], LANES // HEAD_DIM).reshape(1, LANES)
        kg = jnp.tile(k_norm_g[l][_HEAD_PERM], LANES // HEAD_DIM).reshape(1, LANES)
        qkv, vt = _attn_prep(ua, cos_t, sin_t, qg, kg, grp, n_ctx)
        o_a = _global_attn(qkv, vt, n_ctx)
        o_w = _window_attn(qkv, sink_logit[l][q_order], n_ctx)
        r_, v_, al_, lw_, kd_, be_, bonus_, gate_ = _rwkv_prep(
            ur, rwkv_conv[l], rwkv_w0[l].reshape(1, 2 * cw), _block_diag2(rwkv_w2[l]),
            rwkv_a0[l].reshape(1, 2 * cw), _block_diag2(rwkv_a2[l]), rwkv_g2[l],
            rwkv_k_k[l].reshape(1, cw), rwkv_k_a[l].reshape(1, cw), rwkv_r_k[l].reshape(2, cw), grp, n_ctx)
        o_f, o_b = _rwkv_scan(r_, v_, al_, lw_, kd_, be_, n_ctx)
        o_r = _rwkv_out(o_f, o_b, bonus_, gate_, rwkv_ln_w[l].reshape(1, cw), rwkv_ln_b[l].reshape(1, cw), grp, n_ctx)
        h, f = _proj_out(h, o_a, o_w, o_r, w_out[l][_OUT_ROW_PERM].astype(BF16), modsel, norm_ffn_g[l], n_ctx)
        h = _moe(f, h, modsel, wr_t, bias, e_gate[l].astype(BF16), e_up[l].astype(BF16), e_down[l].astype(BF16), n_ctx)
    return _final_norm(h, final_norm_g, n_ctx)
```

```python
import functools

import numpy as np
import jax
import jax.numpy as jnp
from jax import lax
from jax.experimental import pallas as pl
from jax.experimental.pallas import tpu as pltpu

D_MODEL = 1024
GRID_W = 64
HEAD_DIM = 64
ATT_HEADS = 4
ATT_KV_HEADS = 2
RWKV_HEADS = 8
RWKV_WIDTH = RWKV_HEADS * HEAD_DIM
WINDOW = 128
Q_BLOCK = 128
ROPE_THETA = 10000.0
DECAY_LORA = 64
ICLR_LORA = 64
GATE_LORA = 128
N_EXPERTS = 16
N_EXPERT_GROUPS = 4
N_MOD = 6
NORM_EPS = 1e-6
GN_EPS = 64e-5
ATT_COLS = 1024
RWKV_COLS = 3 * RWKV_WIDTH + 2 * DECAY_LORA + 2 * ICLR_LORA + GATE_LORA

LANES = 128
VMEM_LIMIT = 56 * 1024 * 1024

F32 = jnp.float32
BF16 = jnp.bfloat16


def _cparams(*sem):
    return pltpu.CompilerParams(dimension_semantics=sem, vmem_limit_bytes=VMEM_LIMIT)


def _dot(a, b):
    return jnp.dot(a.astype(BF16), b.astype(BF16), preferred_element_type=F32)


def _dot_nt(a, b):
    return lax.dot_general(a.astype(BF16), b.astype(BF16), (((1,), (1,)), ((), ())),
                           preferred_element_type=F32)


def _dot_tn(a, b):
    return lax.dot_general(a.astype(BF16), b.astype(BF16), (((0,), (0,)), ((), ())),
                           preferred_element_type=F32)


def _split_hi_lo(a):
    hi = a.astype(BF16)
    lo = (a - hi.astype(F32)).astype(BF16)
    return hi, lo


def _dot_x3(a, b):
    ah, al = _split_hi_lo(a)
    bh, bl = _split_hi_lo(b)
    return (jnp.dot(ah, bh, preferred_element_type=F32) + jnp.dot(al, bh, preferred_element_type=F32)
            + jnp.dot(ah, bl, preferred_element_type=F32))


def _dot_exact_rhs(a, b_bf16):
    ah, al = _split_hi_lo(a)
    return jnp.dot(ah, b_bf16, preferred_element_type=F32) + jnp.dot(al, b_bf16, preferred_element_type=F32)


def _sigmoid(x):
    return 1.0 / (1.0 + jnp.exp(-x))


def _mod_kernel(c_ref, w_ref, b_ref, o_ref):
    c = c_ref[...]
    s = c * _sigmoid(c)
    o_ref[...] = _dot_x3(s, w_ref[...]) + b_ref[...]


def _modulation(cond, w_mod_l, b_mod_l):
    r, d = cond.shape
    n = w_mod_l.shape[1]
    tn = 1024
    return pl.pallas_call(
        _mod_kernel,
        grid=(n // tn,),
        in_specs=[pl.BlockSpec((r, d), lambda j: (0, 0)),
                  pl.BlockSpec((d, tn), lambda j: (0, j)),
                  pl.BlockSpec((1, tn), lambda j: (0, j))],
        out_specs=pl.BlockSpec((r, tn), lambda j: (0, j)),
        out_shape=jax.ShapeDtypeStruct((r, n), F32),
        compiler_params=_cparams("arbitrary"),
        name="modulation",
    )(cond, w_mod_l, b_mod_l.reshape(1, n))


def _proj_in_kernel(h_ref, mod_ref, g_ref, wa_ref, wr_ref, ua_ref, ur_ref):
    x = h_ref[...]
    y = x * lax.rsqrt(jnp.mean(x * x, axis=-1, keepdims=True) + NORM_EPS) * g_ref[...]
    y = (y * (1.0 + mod_ref[1:2, :]) + mod_ref[0:1, :]).astype(BF16)
    ua_ref[...] = jnp.dot(y, wa_ref[...], preferred_element_type=F32)
    ur_ref[...] = jnp.dot(y, wr_ref[...], preferred_element_type=F32)


def _proj_in(h, modsel, g, wa, wr, n_ctx):
    b, t, d = h.shape
    tm = min(256, n_ctx)
    nct = n_ctx // tm
    return pl.pallas_call(
        _proj_in_kernel,
        grid=(b, t // tm),
        in_specs=[pl.BlockSpec((None, tm, d), lambda bi, i: (bi, i, 0)),
                  pl.BlockSpec((None, None, N_MOD, d), lambda bi, i: (bi, jnp.where(i >= nct, 1, 0), 0, 0)),
                  pl.BlockSpec((1, d), lambda bi, i: (0, 0)),
                  pl.BlockSpec(wa.shape, lambda bi, i: (0, 0)),
                  pl.BlockSpec(wr.shape, lambda bi, i: (0, 0))],
        out_specs=[pl.BlockSpec((None, tm, ATT_COLS), lambda bi, i: (bi, i, 0)),
                   pl.BlockSpec((None, tm, RWKV_COLS), lambda bi, i: (bi, i, 0))],
        out_shape=[jax.ShapeDtypeStruct((b, t, ATT_COLS), F32),
                   jax.ShapeDtypeStruct((b, t, RWKV_COLS), F32)],
        compiler_params=_cparams("parallel", "parallel"),
        name="proj_in",
    )(h, modsel, g.reshape(1, d), wa, wr)


def _swap_halves(x):
    lane = lax.broadcasted_iota(jnp.int32, x.shape, 1)
    fwd = pltpu.roll(x, LANES - HEAD_DIM // 2, 1)
    bwd = pltpu.roll(x, HEAD_DIM // 2, 1)
    return jnp.where(lane % HEAD_DIM < HEAD_DIM // 2, fwd, bwd)


def _attn_prep_kernel(ua_ref, cos_ref, sin_ref, qg_ref, kg_ref, grp_ref, o_ref, vt_ref):
    cos, sin = cos_ref[...], sin_ref[...]
    vt_ref[...] = ua_ref[:, 3 * LANES:4 * LANES].T.astype(BF16)
    grp = grp_ref[...]
    scale = HEAD_DIM ** -0.5

    def head_norm(x, g):
        ss = _dot_exact_rhs(x * x, grp) * (1.0 / HEAD_DIM)
        return x * lax.rsqrt(ss + NORM_EPS) * g

    def rope(x):
        return x * cos + _swap_halves(x) * sin

    def blk(c):
        return ua_ref[:, c * LANES:(c + 1) * LANES]

    def put(c, v):
        o_ref[:, c * LANES:(c + 1) * LANES] = v.astype(BF16)

    put(0, rope(head_norm(blk(0), qg_ref[...])) * scale)
    put(1, rope(head_norm(blk(1), qg_ref[...])) * scale)
    put(2, rope(head_norm(blk(2), kg_ref[...])))
    put(3, blk(3))
    put(4, rope(blk(4)) * scale)
    put(5, rope(blk(5)) * scale)
    put(6, rope(blk(6)))
    put(7, blk(7))


def _attn_prep(ua, cos_t, sin_t, qg, kg, grp, n_ctx):
    b, t, _ = ua.shape
    tt = min(256, n_ctx)
    return pl.pallas_call(
        _attn_prep_kernel,
        grid=(b, t // tt),
        in_specs=[pl.BlockSpec((None, tt, ATT_COLS), lambda bi, i: (bi, i, 0)),
                  pl.BlockSpec((tt, LANES), lambda bi, i: (i, 0)),
                  pl.BlockSpec((tt, LANES), lambda bi, i: (i, 0)),
                  pl.BlockSpec((1, LANES), lambda bi, i: (0, 0)),
                  pl.BlockSpec((1, LANES), lambda bi, i: (0, 0)),
                  pl.BlockSpec((LANES, LANES), lambda bi, i: (0, 0))],
        out_specs=[pl.BlockSpec((None, tt, ATT_COLS), lambda bi, i: (bi, i, 0)),
                   pl.BlockSpec((None, None, LANES, tt), lambda bi, i: (bi, i, 0, 0))],
        out_shape=[jax.ShapeDtypeStruct((b, t, ATT_COLS), BF16),
                   jax.ShapeDtypeStruct((b, t // tt, LANES, tt), BF16)],
        compiler_params=_cparams("parallel", "parallel"),
        name="attn_prep",
    )(ua, cos_t, sin_t, qg, kg, grp)


def _global_attn_kernel(q_ref, k_ref, vt_ref, o_ref, m_scr, acc_scr, *, tk, n_ctx_blocks, n_kv_ctx, n_kv_all):
    i = pl.program_id(1)
    tq = q_ref.shape[0]
    row = lax.broadcasted_iota(jnp.int32, (LANES, tq), 0)
    top = row < HEAD_DIM
    row_v = lax.broadcasted_iota(jnp.int32, (LANES, tk), 0) < HEAD_DIM
    qts = []
    for blk in range(2):
        qt = q_ref[:, blk * LANES:(blk + 1) * LANES].astype(F32).T
        qts.append(jnp.where(top, qt, 0.0).astype(BF16))
        qts.append(jnp.where(top, 0.0, qt).astype(BF16))
    m_scr[...] = jnp.full(m_scr.shape, -jnp.inf, F32)
    acc_scr[...] = jnp.zeros(acc_scr.shape, F32)
    one = jnp.ones((), BF16)

    def scores(chunks):
        kcs = [k_ref[c * tk:(c + 1) * tk, :] for c in chunks]
        return [[jnp.dot(kc, qts[j], preferred_element_type=F32) for kc in kcs] for j in range(4)]

    def finish(chunks, sts):
        vts = [(jnp.where(row_v, vt_ref[c], one), jnp.where(row_v, one, vt_ref[c])) for c in chunks]
        ps, alphas = [], []
        for j in range(4):
            m_old = m_scr[j]
            m_new = m_old
            for st in sts[j]:
                m_new = jnp.maximum(m_new, jnp.max(st, axis=0, keepdims=True))
            ps.append([jnp.exp(st - m_new).astype(BF16) for st in sts[j]])
            alphas.append(jnp.exp(m_old - m_new))
            m_scr[j] = m_new
        pvs = [sum(jnp.dot(vts[n][j % 2], ps[j][n], preferred_element_type=F32) for n in range(len(chunks)))
               for j in range(4)]
        for j in range(4):
            acc_scr[j] = alphas[j] * acc_scr[j] + pvs[j]

    def attend(groups):
        sts = scores(groups[0])
        for g in range(len(groups)):
            nxt = scores(groups[g + 1]) if g + 1 < len(groups) else None
            finish(groups[g], sts)
            sts = nxt

    group = 4
    first = group + n_kv_all % group if n_kv_all > group else n_kv_all
    all_groups = [list(range(first))] + [list(range(s, s + group)) for s in range(first, n_kv_all, group)]

    @pl.when(i < n_ctx_blocks)
    def _():
        attend([list(range(n_kv_ctx))])

    @pl.when(i >= n_ctx_blocks)
    def _():
        attend(all_groups)

    for blk in range(2):
        a0, a1 = acc_scr[2 * blk], acc_scr[2 * blk + 1]
        ot = jnp.where(top, a0 / a0[HEAD_DIM:HEAD_DIM + 1, :], a1 / a1[0:1, :])
        o_ref[:, blk * LANES:(blk + 1) * LANES] = ot.T.astype(BF16)


def _global_attn(qkv, vt, n_ctx):
    b, t, _ = qkv.shape
    tq = min(256, n_ctx)
    tk = vt.shape[3]
    kern = functools.partial(_global_attn_kernel, tk=tk, n_ctx_blocks=n_ctx // tq,
                             n_kv_ctx=n_ctx // tk, n_kv_all=t // tk)
    return pl.pallas_call(
        kern,
        grid=(b, t // tq),
        in_specs=[pl.BlockSpec((None, tq, 2 * LANES), lambda bi, i: (bi, i, 0)),
                  pl.BlockSpec((None, t, LANES), lambda bi, i: (bi, 0, 2)),
                  pl.BlockSpec((None, t // tk, LANES, tk), lambda bi, i: (bi, 0, 0, 0))],
        out_specs=pl.BlockSpec((None, tq, 2 * LANES), lambda bi, i: (bi, i, 0)),
        out_shape=jax.ShapeDtypeStruct((b, t, 2 * LANES), BF16),
        scratch_shapes=[pltpu.VMEM((4, 1, tq), F32), pltpu.VMEM((4, LANES, tq), F32)],
        compiler_params=_cparams("parallel", "arbitrary"),
        name="global_attn",
    )(qkv, qkv, vt)


def _window_attn_kernel(sink_ref, q_ref, k_ref, v_ref, o_ref, *, n_ctx, seq):
    i = pl.program_id(1)
    qb = Q_BLOCK
    band = 3 * qb
    n_ctx_blocks = n_ctx // qb
    is_lat = i >= n_ctx_blocks
    n = i - n_ctx_blocks
    start = jnp.clip((n - 1) * qb, 0, seq - band)
    off = pl.multiple_of(start + n_ctx, qb)
    kc, vc = k_ref[0:n_ctx, :], v_ref[0:n_ctx, :]
    kb, vb = k_ref[pl.ds(off, band), :], v_ref[pl.ds(off, band), :]
    qpos = n * qb + lax.broadcasted_iota(jnp.int32, (qb, band), 0)
    kpos = start + lax.broadcasted_iota(jnp.int32, (qb, band), 1)
    near = (jnp.abs(kpos - qpos) <= WINDOW) & is_lat
    lane = lax.broadcasted_iota(jnp.int32, (qb, LANES), 1)
    low = lane < HEAD_DIM
    zero = jnp.zeros((), BF16)
    qs = []
    for j in range(4):
        blk = q_ref[:, (j // 2) * LANES:(j // 2 + 1) * LANES]
        qs.append(jnp.where(low if j % 2 == 0 else ~low, blk, zero))
    scores = [(_dot_nt(q, kc), _dot_nt(q, kb)) for q in qs]
    probs, dens = [], []
    for j in range(4):
        sink = sink_ref[j]
        s_c = scores[j][0]
        s_b = jnp.where(near, scores[j][1], -jnp.inf)
        m = jnp.maximum(jnp.maximum(jnp.max(s_c, axis=-1, keepdims=True), jnp.max(s_b, axis=-1, keepdims=True)), sink)
        p_c = jnp.exp(s_c - m)
        p_b = jnp.exp(s_b - m)
        dens.append(jnp.sum(p_c, axis=-1, keepdims=True) + jnp.sum(p_b, axis=-1, keepdims=True) + jnp.exp(sink - m))
        probs.append((p_c.astype(BF16), p_b.astype(BF16)))
    accs = [jnp.dot(p_c, vc, preferred_element_type=F32) + jnp.dot(p_b, vb, preferred_element_type=F32)
            for p_c, p_b in probs]
    outs = [acc / den for acc, den in zip(accs, dens)]
    for blk in range(2):
        o_ref[:, blk * LANES:(blk + 1) * LANES] = jnp.where(low, outs[2 * blk], outs[2 * blk + 1]).astype(BF16)


def _window_attn(qkv, sink, n_ctx):
    b, t, _ = qkv.shape
    kern = functools.partial(_window_attn_kernel, n_ctx=n_ctx, seq=t - n_ctx)
    return pl.pallas_call(
        kern,
        grid_spec=pltpu.PrefetchScalarGridSpec(
            num_scalar_prefetch=1,
            grid=(b, t // Q_BLOCK),
            in_specs=[pl.BlockSpec((None, Q_BLOCK, 2 * LANES), lambda bi, i, s: (bi, i, 2)),
                      pl.BlockSpec((None, t, LANES), lambda bi, i, s: (bi, 0, 6)),
                      pl.BlockSpec((None, t, LANES), lambda bi, i, s: (bi, 0, 7))],
            out_specs=pl.BlockSpec((None, Q_BLOCK, 2 * LANES), lambda bi, i, s: (bi, i, 0)),
        ),
        out_shape=jax.ShapeDtypeStruct((b, t, 2 * LANES), BF16),
        compiler_params=_cparams("parallel", "arbitrary"),
        name="window_attn",
    )(sink, qkv, qkv, qkv)


_HEAD_PERM = np.concatenate([np.arange(0, HEAD_DIM, 2), np.arange(1, HEAD_DIM, 2)])
_Q_HEAD_ORDER = (0, 2, 1, 3)


def _q_cols(base, rotary):
    inner = _HEAD_PERM if rotary else np.arange(HEAD_DIM)
    return np.concatenate([base + h * HEAD_DIM + inner for h in _Q_HEAD_ORDER])


def _kv_cols(base, rotary):
    inner = _HEAD_PERM if rotary else np.arange(HEAD_DIM)
    return np.concatenate([base + h * HEAD_DIM + inner for h in range(ATT_KV_HEADS)])


_ATT_COL_PERM = np.concatenate([
    _q_cols(0, True), _kv_cols(256, True), _kv_cols(384, False),
    _q_cols(512, True), _kv_cols(768, True), _kv_cols(896, False)])
_OUT_ROW_PERM = np.concatenate([_q_cols(0, False), _q_cols(256, False), np.arange(512, 1024)])


def _permute_att_cols(w):
    d = w.shape[0]
    half = HEAD_DIM // 2

    def q_part(x):
        return x.reshape(d, 2, 2, half, 2).transpose(0, 2, 1, 4, 3).reshape(d, ATT_HEADS * HEAD_DIM)

    def k_part(x):
        return x.reshape(d, ATT_KV_HEADS, half, 2).transpose(0, 1, 3, 2).reshape(d, ATT_KV_HEADS * HEAD_DIM)

    return jnp.concatenate([q_part(w[:, 0:256]), k_part(w[:, 256:384]), w[:, 384:512],
                            q_part(w[:, 512:768]), k_part(w[:, 768:896]), w[:, 896:1024]], axis=1)


def _permute_out_rows(w):
    d = w.shape[1]
    q_part = lambda x: x.reshape(2, 2, HEAD_DIM, d).transpose(1, 0, 2, 3).reshape(ATT_HEADS * HEAD_DIM, d)
    return jnp.concatenate([q_part(w[0:256]), q_part(w[256:512]), w[512:]], axis=0)


def _rope_tables(n_ctx, seq):
    rows = seq // GRID_W
    row = jnp.repeat(jnp.arange(rows, dtype=F32), GRID_W)
    col = jnp.tile(jnp.arange(GRID_W, dtype=F32), rows)
    n_freq = HEAD_DIM // 4
    inv = ROPE_THETA ** (-jnp.arange(n_freq, dtype=F32) / n_freq)
    ang = jnp.concatenate([row[:, None] * inv, col[:, None] * inv], axis=-1)
    cos = jnp.concatenate([jnp.ones((n_ctx, HEAD_DIM // 2), F32), jnp.cos(ang)], axis=0)
    sin = jnp.concatenate([jnp.zeros((n_ctx, HEAD_DIM // 2), F32), jnp.sin(ang)], axis=0)
    cos_t = jnp.tile(jnp.concatenate([cos, cos], axis=-1), (1, LANES // HEAD_DIM))
    sin_t = jnp.tile(jnp.concatenate([-sin, sin], axis=-1), (1, LANES // HEAD_DIM))
    return cos_t, sin_t


def _rwkv_prep_kernel(cur_ref, prev_ref, next_ref, conv_ref, w0_ref, w2_ref, a0_ref, a2_ref, g2_ref,
                      kk_ref, ka_ref, rk_ref, grp_ref,
                      r_out, v_out, al_out, lw_out, kd_out, be_out, bonus_out, gate_out, *, n_ctx_tiles, n_tiles):
    i = pl.program_id(1)
    tt = cur_ref.shape[0]
    c = RWKV_WIDTH
    x = cur_ref[...]
    row = lax.broadcasted_iota(jnp.int32, (tt, 1), 0)
    has_prev = jnp.logical_and(i != 0, i != n_ctx_tiles)
    has_next = jnp.logical_and(i != n_ctx_tiles - 1, i != n_tiles - 1)
    halo_prev = jnp.where(has_prev, prev_ref[7:8, :], 0.0)
    halo_next = jnp.where(has_next, next_ref[0:1, :], 0.0)
    x_prev = jnp.where(row == 0, halo_prev, pltpu.roll(x, 1, 0))
    x_next = jnp.where(row == tt - 1, halo_next, pltpu.roll(x, tt - 1, 0))
    t = x_prev * conv_ref[0:1, :] + x * conv_ref[1:2, :] + x_next * conv_ref[2:3, :]
    r, k, v = t[:, 0:c], t[:, c:2 * c], t[:, 2 * c:3 * c]
    wd = t[:, 3 * c:3 * c + 2 * DECAY_LORA]
    ad = t[:, 3 * c + 2 * DECAY_LORA:3 * c + 2 * DECAY_LORA + 2 * ICLR_LORA]
    gd = t[:, 3 * c + 2 * DECAY_LORA + 2 * ICLR_LORA:]
    z = w0_ref[...] + _dot_x3(jnp.tanh(wd), w2_ref[...])
    neg_softplus = -(jnp.maximum(-z, 0.0) + jnp.log(1.0 + jnp.exp(-jnp.abs(z))))
    lw_out[...] = -jnp.exp(neg_softplus - 0.5)
    a = _sigmoid(a0_ref[...] + _dot_x3(ad, a2_ref[...]))
    grp = grp_ref[...]
    kk = k * kk_ref[...]
    ss = jnp.concatenate([_dot_exact_rhs(kk[:, j * LANES:(j + 1) * LANES] * kk[:, j * LANES:(j + 1) * LANES], grp)
                          for j in range(c // LANES)], axis=-1)
    kk = kk / jnp.maximum(jnp.sqrt(ss), 1e-12)
    bonus_dot = jnp.zeros((tt, c), F32)
    for j in range(2):
        a_j = a[:, j * c:(j + 1) * c]
        kd_j = k * (1.0 + (a_j - 1.0) * ka_ref[...])
        kd_out[:, j * c:(j + 1) * c] = kd_j
        be_out[:, j * c:(j + 1) * c] = a_j * kk
        bonus_dot = bonus_dot + r * kd_j * rk_ref[j:j + 1, :]
    bsum = jnp.concatenate([_dot_exact_rhs(bonus_dot[:, j * LANES:(j + 1) * LANES], grp)
                            for j in range(c // LANES)], axis=-1)
    r_out[...] = r
    v_out[...] = v
    al_out[...] = kk
    bonus_out[...] = bsum * v
    gate_out[...] = _dot_x3(_sigmoid(gd), g2_ref[...])


def _rwkv_prep(ur, conv, w0, w2blk, a0, a2blk, g2, k_k, k_a, r_k, grp, n_ctx):
    b, t, cols = ur.shape
    c = RWKV_WIDTH
    tt = min(256, n_ctx)
    n_tiles = t // tt
    per8 = tt // 8
    kern = functools.partial(_rwkv_prep_kernel, n_ctx_tiles=n_ctx // tt, n_tiles=n_tiles)
    const = lambda shape: pl.BlockSpec(shape, lambda bi, i: (0,) * len(shape))
    tok = lambda w: pl.BlockSpec((None, tt, w), lambda bi, i: (bi, i, 0))
    return pl.pallas_call(
        kern,
        grid=(b, n_tiles),
        in_specs=[tok(cols),
                  pl.BlockSpec((None, 8, cols), lambda bi, i: (bi, jnp.maximum(i * per8 - 1, 0), 0)),
                  pl.BlockSpec((None, 8, cols), lambda bi, i: (bi, jnp.minimum((i + 1) * per8, t // 8 - 1), 0)),
                  const((3, cols)), const((1, 2 * c)), const((2 * DECAY_LORA, 2 * c)), const((1, 2 * c)),
                  const((2 * ICLR_LORA, 2 * c)), const((GATE_LORA, c)), const((1, c)), const((1, c)),
                  const((2, c)), const((LANES, LANES))],
        out_specs=[tok(c), tok(c), tok(c), tok(2 * c), tok(2 * c), tok(2 * c), tok(c), tok(c)],
        out_shape=[jax.ShapeDtypeStruct((b, t, w), F32) for w in (c, c, c, 2 * c, 2 * c, 2 * c, c, c)],
        compiler_params=_cparams("parallel", "parallel"),
        name="rwkv_prep",
    )(ur, ur, ur, conv, w0, w2blk, a0, a2blk, g2, k_k, k_a, r_k, grp)


RWKV_CHUNK = 64
INV_BLOCK = 16


def _rwkv_masks():
    n2 = 2 * RWKV_CHUNK
    rr, cc = np.indices((n2, n2))
    same = (rr // RWKV_CHUNK) == (cc // RWKV_CHUNK)
    masks = np.stack([same & (cc < rr), same & (cc <= rr), same & (cc > rr), same & (cc >= rr),
                      (rr // INV_BLOCK) == (cc // INV_BLOCK), rr == cc]).astype(np.float32)
    tr, tc = np.indices((RWKV_CHUNK, RWKV_CHUNK))
    tri = np.stack([tc <= tr, tc >= tr]).astype(np.float32)
    return jnp.asarray(masks), jnp.asarray(tri, BF16)


def _rwkv_scan_kernel(rf_ref, vf_ref, alf_ref, lwf_ref, kdf_ref, bef_ref,
                      rb_ref, vb_ref, alb_ref, lwb_ref, kdb_ref, beb_ref,
                      mask_ref, tri_ref, of_ref, ob_ref, s_scr):
    cs = RWKV_CHUNK

    @pl.when(pl.program_id(1) == 0)
    def _():
        s_scr[...] = jnp.zeros(s_scr.shape, F32)

    blk16, eye = mask_ref[4], mask_ref[5]
    low = lax.broadcasted_iota(jnp.int32, (cs, LANES), 1) < HEAD_DIM
    bf = lambda x: x.astype(BF16)
    mm = lambda a, b: jnp.dot(a, b, preferred_element_type=F32)
    nt = lambda a, b: lax.dot_general(a, b, (((1,), (1,)), ((), ())), preferred_element_type=F32)
    tn = lambda a, b: lax.dot_general(a, b, (((0,), (0,)), ((), ())), preferred_element_type=F32)

    def stack(x):
        return jnp.concatenate([jnp.where(low, x, 0.0), jnp.where(low, 0.0, x)], axis=0)

    chains = []
    for d, (r_ref, v_ref, al_ref, lw_ref, kd_ref, be_ref, o_ref) in enumerate((
            (rf_ref, vf_ref, alf_ref, lwf_ref, kdf_ref, bef_ref, of_ref),
            (rb_ref, vb_ref, alb_ref, lwb_ref, kdb_ref, beb_ref, ob_ref))):
        lw = lw_ref[...]
        lw_hi, lw_lo = _split_hi_lo(lw)
        tri = tri_ref[d]
        cl = mm(tri, lw_hi) + mm(tri, lw_lo)
        tot = jnp.sum(lw, axis=0, keepdims=True)
        e_neg, e_end = jnp.exp(-cl), jnp.exp(tot - cl)
        a_t = al_ref[...] * jnp.exp(cl - lw)
        r_t = r_ref[...] * jnp.exp(cl)
        k_t, b_t = kd_ref[...] * e_neg, be_ref[...] * e_neg
        k_e, b_e = kd_ref[...] * e_end, be_ref[...] * e_end
        p_end = jnp.exp(tot)
        v = v_ref[...]
        for p in range(RWKV_HEADS // 2):
            sl = slice(p * LANES, (p + 1) * LANES)
            xa, xr = stack(a_t[:, sl]), stack(r_t[:, sl])
            chains.append(dict(
                d=d, p=p, sl=sl, o_ref=o_ref, strict=mask_ref[2 * d], incl=mask_ref[2 * d + 1],
                xa=xa, xr=xr, lhs=bf(jnp.concatenate([xa, xr], axis=0)),
                ybk=bf(jnp.concatenate([b_t[:, sl], b_t[:, sl], k_t[:, sl], k_t[:, sl]], axis=0)),
                vs=bf(stack(v[:, sl])), kbs=bf(jnp.concatenate([stack(k_e[:, sl]), stack(b_e[:, sl])], axis=0)),
                pe=p_end[:, sl]))
    n2 = 2 * cs
    for c in chains:
        c["x"] = nt(c["lhs"], c["ybk"])
    for c in chains:
        x = c["x"]
        l_m = x[:n2, :n2] * c["strict"]
        c["m_ak"] = bf(x[:n2, n2:] * c["strict"])
        c["m_r"] = bf(jnp.concatenate([x[n2:, n2:] * c["incl"], x[n2:, :n2] * c["incl"]], axis=1))
        l_d = l_m * blk16
        c["l_d"], c["l_o"] = l_d, bf(l_m - l_d)
    for c in chains:
        ld = bf(c["l_d"])
        c["l2"] = mm(ld, ld)
        c["mv"] = mm(c["m_ak"], c["vs"])
    for c in chains:
        l2 = bf(c["l2"])
        t1 = eye - c["l_d"]
        y = mm(l2, jnp.concatenate([l2, bf(t1)], axis=1))
        c["l4"], c["t2"] = bf(y[:, :n2]), t1 + y[:, n2:]
    for c in chains:
        y = mm(c["l4"], jnp.concatenate([c["l4"], bf(c["t2"])], axis=1))
        c["l8"], c["t4"] = bf(y[:, :n2]), c["t2"] + y[:, n2:]
    for c in chains:
        c["dinv"] = bf(c["t4"] + mm(c["l8"], bf(c["t4"])))
    for c in chains:
        y = mm(c["dinv"], jnp.concatenate([c["l_o"], bf(c["mv"])], axis=1))
        c["n"] = bf(y[:, :n2])
        c["dr"] = jnp.concatenate([y[:, n2:], mm(c["dinv"], bf(c["xa"]))], axis=1)
    for c in chains:
        dr = bf(c["dr"])
        y = mm(c["n"], jnp.concatenate([c["n"], dr[:, :n2]], axis=1))
        c["nsq"] = bf(y[:, :n2])
        c["xx"] = c["dr"] - jnp.concatenate([y[:, n2:], mm(c["n"], dr[:, n2:])], axis=1)
    for c in chains:
        c["uw"] = bf(c["xx"] + mm(c["nsq"], bf(c["xx"])))
    for c in chains:
        rhs = jnp.concatenate([jnp.concatenate([c["vs"], jnp.zeros_like(c["vs"])], axis=1), -c["uw"]], axis=0)
        y = mm(c["m_r"], rhs)
        g = tn(c["kbs"], rhs)
        rqs = c["xr"] + y[:, n2:]
        c["rq"] = bf(rqs[:cs] + rqs[cs:])
        c["y0"] = y[:cs, :n2] + y[cs:, :n2]
        c["m_t"] = bf(eye * c["pe"] + g[:, n2:])
        c["g_t"] = g[:, :n2]
    for c in chains:
        s_old = bf(s_scr[c["d"], c["p"]])
        c["o_ref"][:, c["sl"]] = mm(c["rq"], s_old) + c["y0"]
        s_scr[c["d"], c["p"]] = mm(c["m_t"], s_old) + c["g_t"]


def _rwkv_scan(r, v, al, lw, kd, be, n_ctx):
    b, t, c = r.shape
    cs = RWKV_CHUNK
    ncc, ntot = n_ctx // cs, t // cs
    masks, tri = _rwkv_masks()

    def back(s):
        return jnp.where(s < ncc, ncc - 1 - s, ntot - 1 - s + ncc)

    fwd = lambda col: pl.BlockSpec((None, cs, c), lambda bi, s: (bi, s, col))
    bwd = lambda col: pl.BlockSpec((None, cs, c), lambda bi, s: (bi, back(s), col))
    const = lambda a: pl.BlockSpec(a.shape, lambda bi, s: (0,) * a.ndim)
    return pl.pallas_call(
        _rwkv_scan_kernel,
        grid=(b, ntot),
        in_specs=[fwd(0), fwd(0), fwd(0), fwd(0), fwd(0), fwd(0),
                  bwd(0), bwd(0), bwd(0), bwd(1), bwd(1), bwd(1), const(masks), const(tri)],
        out_specs=[pl.BlockSpec((None, cs, c), lambda bi, s: (bi, s, 0)),
                   pl.BlockSpec((None, cs, c), lambda bi, s: (bi, back(s), 0))],
        out_shape=[jax.ShapeDtypeStruct((b, t, c), F32), jax.ShapeDtypeStruct((b, t, c), F32)],
        scratch_shapes=[pltpu.VMEM((2, RWKV_HEADS // 2, LANES, LANES), F32)],
        compiler_params=_cparams("parallel", "arbitrary"),
        name="rwkv_scan",
    )(r, v, al, lw, kd, be, r, v, al, lw, kd, be, masks, tri)


def _rwkv_out_kernel(of_ref, ob_ref, bonus_ref, gate_ref, lnw_ref, lnb_ref, grp_ref, o_ref):
    grp = grp_ref[...]
    for j in range(RWKV_WIDTH // LANES):
        sl = slice(j * LANES, (j + 1) * LANES)
        o = of_ref[:, sl] + ob_ref[:, sl]
        mu = _dot_exact_rhs(o, grp) * (1.0 / HEAD_DIM)
        d = o - mu
        var = _dot_exact_rhs(d * d, grp) * (1.0 / HEAD_DIM)
        on = d * lax.rsqrt(var + GN_EPS) * lnw_ref[:, sl] + lnb_ref[:, sl]
        o_ref[:, sl] = ((on + bonus_ref[:, sl]) * gate_ref[:, sl]).astype(BF16)


def _rwkv_out(o_f, o_b, bonus, gate, ln_w, ln_b, grp, n_ctx):
    b, t, c = o_f.shape
    tt = min(256, n_ctx)
    tok = pl.BlockSpec((None, tt, c), lambda bi, i: (bi, i, 0))
    const = lambda shape: pl.BlockSpec(shape, lambda bi, i: (0,) * len(shape))
    return pl.pallas_call(
        _rwkv_out_kernel,
        grid=(b, t // tt),
        in_specs=[tok, tok, tok, tok, const((1, c)), const((1, c)), const((LANES, LANES))],
        out_specs=tok,
        out_shape=jax.ShapeDtypeStruct((b, t, c), BF16),
        compiler_params=_cparams("parallel", "parallel"),
        name="rwkv_out",
    )(o_f, o_b, bonus, gate, ln_w, ln_b, grp)


def _block_diag2(w):
    z = jnp.zeros_like(w[0])
    return jnp.concatenate([jnp.concatenate([w[0], z], axis=1), jnp.concatenate([z, w[1]], axis=1)], axis=0)


def _proj_out_kernel(h_ref, oa_ref, ow_ref, or_ref, w_ref, mod_ref, g_ref, hn_ref, f_ref):
    na, nw = oa_ref.shape[1], ow_ref.shape[1]
    mix = (jnp.dot(oa_ref[...], w_ref[0:na, :], preferred_element_type=F32)
           + jnp.dot(ow_ref[...], w_ref[na:na + nw, :], preferred_element_type=F32)
           + jnp.dot(or_ref[...], w_ref[na + nw:, :], preferred_element_type=F32))
    h = h_ref[...] + mod_ref[2:3, :] * mix
    hn_ref[...] = h
    y = h * lax.rsqrt(jnp.mean(h * h, axis=-1, keepdims=True) + NORM_EPS) * g_ref[...]
    f_ref[...] = y * (1.0 + mod_ref[4:5, :]) + mod_ref[3:4, :]


def _proj_out(h, o_a, o_w, o_r, w_out, modsel, g, n_ctx):
    b, t, d = h.shape
    tm = min(256, n_ctx)
    nct = n_ctx // tm
    tok = lambda w: pl.BlockSpec((None, tm, w), lambda bi, i: (bi, i, 0))
    return pl.pallas_call(
        _proj_out_kernel,
        grid=(b, t // tm),
        in_specs=[tok(d), tok(o_a.shape[2]), tok(o_w.shape[2]), tok(o_r.shape[2]),
                  pl.BlockSpec(w_out.shape, lambda bi, i: (0, 0)),
                  pl.BlockSpec((None, None, N_MOD, d), lambda bi, i: (bi, jnp.where(i >= nct, 1, 0), 0, 0)),
                  pl.BlockSpec((1, d), lambda bi, i: (0, 0))],
        out_specs=[tok(d), tok(d)],
        out_shape=[jax.ShapeDtypeStruct((b, t, d), F32), jax.ShapeDtypeStruct((b, t, d), F32)],
        compiler_params=_cparams("parallel", "parallel"),
        name="proj_out",
    )(h, o_a, o_w, o_r, w_out, modsel, g.reshape(1, d))


PAIRS_PER_GROUP = 6
N_PAIR_CLASSES = N_EXPERT_GROUPS * PAIRS_PER_GROUP
_PAIR_LO = (0, 0, 0, 1, 1, 2)
_PAIR_HI = (1, 2, 3, 2, 3, 3)


def _router_kernel(f_ref, wr_ref, bias_ref, cls_ref, gate_ref, rank_ref, cnt_ref):
    tt = f_ref.shape[0]
    per_group = N_EXPERTS // N_EXPERT_GROUPS

    @pl.when(pl.program_id(0) == 0)
    def _():
        cnt_ref[...] = jnp.zeros(cnt_ref.shape, F32)

    logits = _dot_nt(wr_ref[...], f_ref[...])
    score = _sigmoid(logits)
    biased = score + bias_ref[...]
    b = [biased[e:e + 1, :] for e in range(N_EXPERTS)]
    s = [score[e:e + 1, :] for e in range(N_EXPERTS)]
    g_best, g_sel = None, None
    for g in range(N_EXPERT_GROUPS):
        rows = b[g * per_group:(g + 1) * per_group]
        top2 = None
        for i in range(per_group):
            for j in range(i + 1, per_group):
                pair = rows[i] + rows[j]
                top2 = pair if top2 is None else jnp.maximum(top2, pair)
        if g == 0:
            g_best, g_sel = top2, jnp.zeros(top2.shape, jnp.int32)
        else:
            upd = top2 > g_best
            g_best = jnp.where(upd, top2, g_best)
            g_sel = jnp.where(upd, g, g_sel)
    neg = jnp.full(g_best.shape, -jnp.inf, F32)
    m = [jnp.where(g_sel == e // per_group, b[e], neg) for e in range(N_EXPERTS)]

    def first_argmax(vals):
        best, idx = vals[0], jnp.zeros(vals[0].shape, jnp.int32)
        for e in range(1, N_EXPERTS):
            upd = vals[e] > best
            best = jnp.where(upd, vals[e], best)
            idx = jnp.where(upd, e, idx)
        return idx

    e1 = first_argmax(m)
    e2 = first_argmax([jnp.where(e1 == e, neg, m[e]) for e in range(N_EXPERTS)])
    zero = jnp.zeros(g_best.shape, F32)
    s1 = sum(jnp.where(e1 == e, s[e], zero) for e in range(N_EXPERTS))
    s2 = sum(jnp.where(e2 == e, s[e], zero) for e in range(N_EXPERTS))
    den = s1 + s2
    lo, hi = jnp.minimum(e1, e2), jnp.maximum(e1, e2)
    lo_l, hi_l = lo - g_sel * per_group, hi - g_sel * per_group
    pair = jnp.where(lo_l == 0, hi_l - 1, jnp.where(lo_l == 1, hi_l + 1, 5))
    cls = g_sel * PAIRS_PER_GROUP + pair
    hits = jnp.concatenate([jnp.where(cls == k, 1.0, 0.0) for k in range(N_PAIR_CLASSES)], axis=0)
    tr = lax.broadcasted_iota(jnp.int32, (tt, tt), 0)
    tc = lax.broadcasted_iota(jnp.int32, (tt, tt), 1)
    earlier = jnp.where(tr < tc, 1.0, 0.0).astype(BF16)
    before = cnt_ref[:, 0:1] + jnp.dot(hits.astype(BF16), earlier, preferred_element_type=F32)
    cls_ref[...] = cls
    first_is_lo = e1 < e2
    gate_ref[0:1, :] = jnp.where(first_is_lo, s1, s2) / den
    gate_ref[1:2, :] = jnp.where(first_is_lo, s2, s1) / den
    rank_ref[...] = jnp.sum(hits * before, axis=0, keepdims=True).astype(jnp.int32)
    cnt_ref[...] = cnt_ref[...] + jnp.sum(hits, axis=1, keepdims=True)


def _router(f2, wr_t, bias):
    n, d = f2.shape
    tt = int(np.gcd(n, 512))
    row = lambda r: pl.BlockSpec((r, tt), lambda i: (0, i))
    return pl.pallas_call(
        _router_kernel,
        grid=(n // tt,),
        in_specs=[pl.BlockSpec((tt, d), lambda i: (i, 0)),
                  pl.BlockSpec((N_EXPERTS, d), lambda i: (0, 0)),
                  pl.BlockSpec((N_EXPERTS, 1), lambda i: (0, 0))],
        out_specs=[row(1), row(2), row(1), pl.BlockSpec((N_PAIR_CLASSES, LANES), lambda i: (0, 0))],
        out_shape=[jax.ShapeDtypeStruct((1, n), jnp.int32), jax.ShapeDtypeStruct((2, n), F32),
                   jax.ShapeDtypeStruct((1, n), jnp.int32), jax.ShapeDtypeStruct((N_PAIR_CLASSES, LANES), F32)],
        compiler_params=_cparams("arbitrary"),
        name="router",
    )(f2, wr_t, bias)


MOE_BLOCK = 256
DMA_UNROLL = 16


def _dispatch_kernel(slot_ref, f_ref, init_hbm, xs_hbm, sem):
    del init_hbm
    td = slot_ref.shape[1]

    def start(r, carry):
        pltpu.make_async_copy(f_ref.at[pl.ds(r, 1)], xs_hbm.at[pl.ds(slot_ref[0, r], 1)], sem).start()
        return carry

    lax.fori_loop(0, td, start, 0, unroll=DMA_UNROLL)
    pltpu.make_async_copy(f_ref, xs_hbm.at[pl.ds(0, td)], sem).wait()


def _dispatch(f2, slot, n_slots):
    n, d = f2.shape
    td = int(np.gcd(n, 1024))
    return pl.pallas_call(
        _dispatch_kernel,
        grid=(n // td,),
        in_specs=[pl.BlockSpec((1, td), lambda i: (0, i), memory_space=pltpu.SMEM),
                  pl.BlockSpec((td, d), lambda i: (i, 0)),
                  pl.BlockSpec(memory_space=pl.ANY)],
        out_specs=pl.BlockSpec(memory_space=pl.ANY),
        out_shape=jax.ShapeDtypeStruct((n_slots, d), F32),
        scratch_shapes=[pltpu.SemaphoreType.DMA(())],
        input_output_aliases={2: 0},
        compiler_params=_cparams("arbitrary"),
        name="moe_dispatch",
    )(slot, f2, jnp.zeros((n_slots, d), F32))


def _ffn_kernel(e_lo_ref, e_hi_ref, n_used_ref, x_ref, wg0_ref, wu0_ref, wd0_ref, wg1_ref, wu1_ref, wd1_ref, y_ref):
    del e_lo_ref, e_hi_ref
    d = x_ref.shape[1]

    @pl.when(pl.program_id(0) < n_used_ref[0])
    def _():
        x = x_ref[...].astype(BF16)
        for k, (wg_ref, wu_ref, wd_ref) in enumerate(((wg0_ref, wu0_ref, wd0_ref), (wg1_ref, wu1_ref, wd1_ref))):
            hg = jnp.dot(x, wg_ref[...], preferred_element_type=F32)
            hu = jnp.dot(x, wu_ref[...], preferred_element_type=F32)
            act = (hg * _sigmoid(hg) * hu).astype(BF16)
            y_ref[:, k * d:(k + 1) * d] = jnp.dot(act, wd_ref[...], preferred_element_type=F32)

    @pl.when(pl.program_id(0) >= n_used_ref[0])
    def _():
        y_ref[...] = jnp.zeros(y_ref.shape, F32)


def _expert_ffn(xs, e_lo, e_hi, n_used, wg, wu, wd):
    n_slots, d = xs.shape
    ff = wg.shape[2]
    bm = MOE_BLOCK
    last = lambda j, nu: jnp.minimum(j, nu[0] - 1)
    w_in = lambda which: pl.BlockSpec((None, d, ff), lambda j, lo, hi, nu: ((lo, hi)[which][last(j, nu)], 0, 0))
    w_out = lambda which: pl.BlockSpec((None, ff, d), lambda j, lo, hi, nu: ((lo, hi)[which][last(j, nu)], 0, 0))
    return pl.pallas_call(
        _ffn_kernel,
        grid_spec=pltpu.PrefetchScalarGridSpec(
            num_scalar_prefetch=3,
            grid=(n_slots // bm,),
            in_specs=[pl.BlockSpec((bm, d), lambda j, lo, hi, nu: (last(j, nu), 0)),
                      w_in(0), w_in(0), w_out(0), w_in(1), w_in(1), w_out(1)],
            out_specs=pl.BlockSpec((bm, 2 * d), lambda j, lo, hi, nu: (j, 0)),
        ),
        out_shape=jax.ShapeDtypeStruct((n_slots, 2 * d), F32),
        compiler_params=_cparams("arbitrary"),
        name="expert_ffn",
    )(e_lo, e_hi, n_used, xs, wg, wu, wd, wg, wu, wd)


def _combine_kernel(slot_ref, ys_hbm, h_ref, gate_ref, mod_ref, o_ref, buf, sem):
    tc, d = h_ref.shape

    def start(r, carry):
        pltpu.make_async_copy(ys_hbm.at[pl.ds(slot_ref[0, r], 1)], buf.at[pl.ds(r, 1)], sem).start()
        return carry

    lax.fori_loop(0, tc, start, 0, unroll=DMA_UNROLL)
    pltpu.make_async_copy(ys_hbm.at[pl.ds(0, tc)], buf, sem).wait()
    y = gate_ref[:, 0:1] * buf[:, 0:d] + gate_ref[:, 1:2] * buf[:, d:2 * d]
    o_ref[...] = h_ref[...] + mod_ref[5:6, :] * y


def _combine(ys, slot, gates_t, h, modsel, n_ctx):
    b, t, d = h.shape
    tc = min(256, n_ctx)
    nct = n_ctx // tc
    per_b = t // tc
    return pl.pallas_call(
        _combine_kernel,
        grid=(b, per_b),
        in_specs=[pl.BlockSpec((1, tc), lambda bi, i: (0, bi * per_b + i), memory_space=pltpu.SMEM),
                  pl.BlockSpec(memory_space=pl.ANY),
                  pl.BlockSpec((None, tc, d), lambda bi, i: (bi, i, 0)),
                  pl.BlockSpec((tc, 2), lambda bi, i: (bi * per_b + i, 0)),
                  pl.BlockSpec((None, None, N_MOD, d), lambda bi, i: (bi, jnp.where(i >= nct, 1, 0), 0, 0))],
        out_specs=pl.BlockSpec((None, tc, d), lambda bi, i: (bi, i, 0)),
        out_shape=jax.ShapeDtypeStruct((b, t, d), F32),
        scratch_shapes=[pltpu.VMEM((tc, 2 * d), F32), pltpu.SemaphoreType.DMA(())],
        compiler_params=_cparams("arbitrary", "arbitrary"),
        name="moe_combine",
    )(slot, ys, h, gates_t, modsel)


def _moe(f, h, modsel, wr_t, bias, wg, wu, wd, n_ctx):
    b, t, d = f.shape
    n = b * t
    f2 = f.reshape(n, d)
    cls, gates, rank, cnt = _router(f2, wr_t, bias)
    bm = MOE_BLOCK
    counts = cnt[:, 0].astype(jnp.int32)
    padded = (counts + bm - 1) // bm * bm
    pend = jnp.cumsum(padded)
    pstart = pend - padded
    slot = pstart[cls] + rank
    n_slots = -(-n // bm) * bm + N_PAIR_CLASSES * bm
    n_blocks = n_slots // bm
    blk_start = jnp.arange(n_blocks, dtype=jnp.int32) * bm
    blk_cls = jnp.minimum(jnp.sum(pend[None, :] <= blk_start[:, None], axis=1), N_PAIR_CLASSES - 1).astype(jnp.int32)
    per_group = N_EXPERTS // N_EXPERT_GROUPS
    blk_grp, blk_pair = blk_cls // PAIRS_PER_GROUP, blk_cls % PAIRS_PER_GROUP
    e_lo = blk_grp * per_group + jnp.asarray(_PAIR_LO, jnp.int32)[blk_pair]
    e_hi = blk_grp * per_group + jnp.asarray(_PAIR_HI, jnp.int32)[blk_pair]
    n_used = (pend[-1:] // bm).astype(jnp.int32)
    xs = _dispatch(f2, slot, n_slots)
    ys = _expert_ffn(xs, e_lo, e_hi, n_used, wg, wu, wd)
    return _combine(ys, slot, gates.T, h, modsel, n_ctx)


def _final_norm_kernel(h_ref, g_ref, o_ref):
    x = h_ref[...]
    o_ref[...] = x * lax.rsqrt(jnp.mean(x * x, axis=-1, keepdims=True) + NORM_EPS) * g_ref[...]


def _final_norm(h, g, n_ctx):
    b, t, d = h.shape
    tm = min(256, n_ctx)
    skip = n_ctx // tm
    return pl.pallas_call(
        _final_norm_kernel,
        grid=(b, (t - n_ctx) // tm),
        in_specs=[pl.BlockSpec((None, tm, d), lambda bi, i: (bi, i + skip, 0)),
                  pl.BlockSpec((1, d), lambda bi, i: (0, 0))],
        out_specs=pl.BlockSpec((None, tm, d), lambda bi, i: (bi, i, 0)),
        out_shape=jax.ShapeDtypeStruct((b, t - n_ctx, d), F32),
        compiler_params=_cparams("parallel", "parallel"),
        name="final_norm",
    )(h, g.reshape(1, d))


def kernel(x, c, ctx, c_ctx, w_mod, b_mod, norm_mix_g, norm_ffn_g, w_in, q_norm_g, k_norm_g, sink_logit,
           rwkv_conv, rwkv_w0, rwkv_w2, rwkv_a0, rwkv_a2, rwkv_g2, rwkv_k_k, rwkv_k_a, rwkv_r_k, rwkv_ln_w,
           rwkv_ln_b, w_out, w_router, router_bias, e_gate, e_up, e_down, final_norm_g):
    b, seq, d = x.shape
    n_ctx = ctx.shape[1]
    depth = w_mod.shape[0]
    cw = RWKV_WIDTH
    h = jnp.concatenate([ctx, x], axis=1)
    cond_rows = -(-(b + 1) // 8) * 8
    cond = jnp.zeros((cond_rows, d), F32).at[:b].set(c).at[b].set(c_ctx)
    cos_t, sin_t = _rope_tables(n_ctx, seq)
    grp = jnp.asarray(np.kron(np.eye(LANES // HEAD_DIM), np.ones((HEAD_DIM, HEAD_DIM))), BF16)
    wr_t = w_router.T
    bias = router_bias.reshape(N_EXPERTS, 1)
    q_order = jnp.asarray(_Q_HEAD_ORDER)
    for l in range(depth):
        mods = _modulation(cond, w_mod[l], b_mod[l])
        modsel = jnp.stack([jnp.broadcast_to(mods[b], (b, N_MOD * d)), mods[:b]], axis=1).reshape(b, 2, N_MOD, d)
        wa = _permute_att_cols(w_in[l][:, :ATT_COLS]).astype(BF16)
        wr = w_in[l][:, ATT_COLS:].astype(BF16)
        ua, ur = _proj_in(h, modsel, norm_mix_g[l], wa, wr, n_ctx)
        qg = jnp.tile(q_norm_g[l][_HEAD_PERM], LANES // HEAD_DIM).reshape(1, LANES)
        kg = jnp.tile(k_norm_g[l][_HEAD_PERM], LANES // HEAD_DIM).reshape(1, LANES)
        qkv, vt = _attn_prep(ua, cos_t, sin_t, qg, kg, grp, n_ctx)
        o_a = _global_attn(qkv, vt, n_ctx)
        o_w = _window_attn(qkv, sink_logit[l][q_order], n_ctx)
        r_, v_, al_, lw_, kd_, be_, bonus_, gate_ = _rwkv_prep(
            ur, rwkv_conv[l], rwkv_w0[l].reshape(1, 2 * cw), _block_diag2(rwkv_w2[l]),
            rwkv_a0[l].reshape(1, 2 * cw), _block_diag2(rwkv_a2[l]), rwkv_g2[l],
            rwkv_k_k[l].reshape(1, cw), rwkv_k_a[l].reshape(1, cw), rwkv_r_k[l].reshape(2, cw), grp, n_ctx)
        o_f, o_b = _rwkv_scan(r_, v_, al_, lw_, kd_, be_, n_ctx)
        o_r = _rwkv_out(o_f, o_b, bonus_, gate_, rwkv_ln_w[l].reshape(1, cw), rwkv_ln_b[l].reshape(1, cw), grp, n_ctx)
        h, f = _proj_out(h, o_a, o_w, o_r, _permute_out_rows(w_out[l]).astype(BF16), modsel, norm_ffn_g[l], n_ctx)
        h = _moe(f, h, modsel, wr_t, bias, e_gate[l].astype(BF16), e_up[l].astype(BF16), e_down[l].astype(BF16), n_ctx)
    return _final_norm(h, final_norm_g, n_ctx)
```

```python
import functools

import numpy as np
import jax
import jax.numpy as jnp
from jax import lax
from jax.experimental import pallas as pl
from jax.experimental.pallas import tpu as pltpu

D_MODEL = 1024
GRID_W = 64
HEAD_DIM = 64
ATT_HEADS = 4
ATT_KV_HEADS = 2
RWKV_HEADS = 8
RWKV_WIDTH = RWKV_HEADS * HEAD_DIM
WINDOW = 128
Q_BLOCK = 128
ROPE_THETA = 10000.0
DECAY_LORA = 64
ICLR_LORA = 64
GATE_LORA = 128
N_EXPERTS = 16
N_EXPERT_GROUPS = 4
N_MOD = 6
NORM_EPS = 1e-6
GN_EPS = 64e-5
ATT_COLS = 1024
RWKV_COLS = 3 * RWKV_WIDTH + 2 * DECAY_LORA + 2 * ICLR_LORA + GATE_LORA

LANES = 128
VMEM_LIMIT = 56 * 1024 * 1024

F32 = jnp.float32
BF16 = jnp.bfloat16


def _cparams(*sem):
    return pltpu.CompilerParams(dimension_semantics=sem, vmem_limit_bytes=VMEM_LIMIT)


def _dot(a, b):
    return jnp.dot(a.astype(BF16), b.astype(BF16), preferred_element_type=F32)


def _dot_nt(a, b):
    return lax.dot_general(a.astype(BF16), b.astype(BF16), (((1,), (1,)), ((), ())),
                           preferred_element_type=F32)


def _dot_tn(a, b):
    return lax.dot_general(a.astype(BF16), b.astype(BF16), (((0,), (0,)), ((), ())),
                           preferred_element_type=F32)


def _split_hi_lo(a):
    hi = a.astype(BF16)
    lo = (a - hi.astype(F32)).astype(BF16)
    return hi, lo


def _dot_x3(a, b):
    ah, al = _split_hi_lo(a)
    bh, bl = _split_hi_lo(b)
    return (jnp.dot(ah, bh, preferred_element_type=F32) + jnp.dot(al, bh, preferred_element_type=F32)
            + jnp.dot(ah, bl, preferred_element_type=F32))


def _dot_exact_rhs(a, b_bf16):
    ah, al = _split_hi_lo(a)
    return jnp.dot(ah, b_bf16, preferred_element_type=F32) + jnp.dot(al, b_bf16, preferred_element_type=F32)


def _sigmoid(x):
    return 1.0 / (1.0 + jnp.exp(-x))


def _mod_kernel(c_ref, w_ref, b_ref, o_ref):
    c = c_ref[...]
    s = c * _sigmoid(c)
    o_ref[...] = _dot_x3(s, w_ref[...]) + b_ref[...]


def _modulation(cond, w_mod_l, b_mod_l):
    r, d = cond.shape
    n = w_mod_l.shape[1]
    tn = 1024
    return pl.pallas_call(
        _mod_kernel,
        grid=(n // tn,),
        in_specs=[pl.BlockSpec((r, d), lambda j: (0, 0)),
                  pl.BlockSpec((d, tn), lambda j: (0, j)),
                  pl.BlockSpec((1, tn), lambda j: (0, j))],
        out_specs=pl.BlockSpec((r, tn), lambda j: (0, j)),
        out_shape=jax.ShapeDtypeStruct((r, n), F32),
        compiler_params=_cparams("arbitrary"),
        name="modulation",
    )(cond, w_mod_l, b_mod_l.reshape(1, n))


def _swap_halves(x):
    lane = lax.broadcasted_iota(jnp.int32, x.shape, 1)
    fwd = pltpu.roll(x, LANES - HEAD_DIM // 2, 1)
    bwd = pltpu.roll(x, HEAD_DIM // 2, 1)
    return jnp.where(lane % HEAD_DIM < HEAD_DIM // 2, fwd, bwd)


def _stream_specs(tm, d, n_ctx_tiles, x_off):
    return [pl.BlockSpec((None, tm, d), lambda bi, i: (bi, jnp.minimum(i, n_ctx_tiles - 1), 0)),
            pl.BlockSpec((None, tm, d), lambda bi, i: (bi, jnp.maximum(i - n_ctx_tiles, 0) + x_off, 0))]


def _stream_tile(hc_ref, hx_ref, n_ctx_tiles):
    return jnp.where(pl.program_id(1) < n_ctx_tiles, hc_ref[...], hx_ref[...])


def _proj_in_kernel(hc_ref, hx_ref, mod_ref, g_ref, wa_ref, wr_ref, cos_ref, sin_ref, qg_ref, kg_ref, grp_ref,
                    qkv_ref, vt_ref, ur_ref, *, n_ctx_tiles):
    x = _stream_tile(hc_ref, hx_ref, n_ctx_tiles)
    y = x * lax.rsqrt(jnp.mean(x * x, axis=-1, keepdims=True) + NORM_EPS) * g_ref[...]
    y = (y * (1.0 + mod_ref[1:2, :]) + mod_ref[0:1, :]).astype(BF16)
    ur_ref[...] = jnp.dot(y, wr_ref[...], preferred_element_type=F32)
    cos, sin = cos_ref[...], sin_ref[...]
    grp = grp_ref[...]
    scale = HEAD_DIM ** -0.5

    def head_norm(v, g):
        ss = _dot_exact_rhs(v * v, grp) * (1.0 / HEAD_DIM)
        return v * lax.rsqrt(ss + NORM_EPS) * g

    def rope(v):
        return v * cos + _swap_halves(v) * sin

    def blk(c):
        return jnp.dot(y, wa_ref[:, c * LANES:(c + 1) * LANES], preferred_element_type=F32)

    def put(c, v):
        qkv_ref[:, c * LANES:(c + 1) * LANES] = v.astype(BF16)

    put(0, rope(head_norm(blk(0), qg_ref[...])) * scale)
    put(1, rope(head_norm(blk(1), qg_ref[...])) * scale)
    put(2, rope(head_norm(blk(2), kg_ref[...])))
    v_glob = blk(3)
    put(3, v_glob)
    vt_ref[...] = v_glob.T.astype(BF16)
    put(4, rope(blk(4)) * scale)
    put(5, rope(blk(5)) * scale)
    put(6, rope(blk(6)))
    put(7, blk(7))


def _proj_in(stream, t, modsel, g, wa, wr, cos_t, sin_t, qg, kg, grp, n_ctx):
    hc, hx, x_off = stream
    b, _, d = hc.shape
    tm = min(256, n_ctx)
    nct = n_ctx // tm
    const = lambda shape: pl.BlockSpec(shape, lambda bi, i: (0,) * len(shape))
    return pl.pallas_call(
        functools.partial(_proj_in_kernel, n_ctx_tiles=nct),
        grid=(b, t // tm),
        in_specs=_stream_specs(tm, d, nct, x_off) + [
                  pl.BlockSpec((None, None, N_MOD, d), lambda bi, i: (bi, jnp.where(i >= nct, 1, 0), 0, 0)),
                  const((1, d)), const(wa.shape), const(wr.shape),
                  pl.BlockSpec((tm, LANES), lambda bi, i: (i, 0)),
                  pl.BlockSpec((tm, LANES), lambda bi, i: (i, 0)),
                  const((1, LANES)), const((1, LANES)), const((LANES, LANES))],
        out_specs=[pl.BlockSpec((None, tm, ATT_COLS), lambda bi, i: (bi, i, 0)),
                   pl.BlockSpec((None, None, LANES, tm), lambda bi, i: (bi, i, 0, 0)),
                   pl.BlockSpec((None, tm, RWKV_COLS), lambda bi, i: (bi, i, 0))],
        out_shape=[jax.ShapeDtypeStruct((b, t, ATT_COLS), BF16),
                   jax.ShapeDtypeStruct((b, t // tm, LANES, tm), BF16),
                   jax.ShapeDtypeStruct((b, t, RWKV_COLS), F32)],
        compiler_params=_cparams("parallel", "parallel"),
        name="proj_in",
    )(hc, hx, modsel, g.reshape(1, d), wa, wr, cos_t, sin_t, qg, kg, grp)


def _global_attn_kernel(q_ref, k_ref, vt_ref, o_ref, m_scr, acc_scr, *, tk, n_ctx_blocks, n_kv_ctx, n_kv_all):
    i = pl.program_id(1)
    tq = q_ref.shape[0]
    row = lax.broadcasted_iota(jnp.int32, (LANES, tq), 0)
    top = row < HEAD_DIM
    row_v = lax.broadcasted_iota(jnp.int32, (LANES, tk), 0) < HEAD_DIM
    qts = []
    for blk in range(2):
        qt = q_ref[:, blk * LANES:(blk + 1) * LANES].astype(F32).T
        qts.append(jnp.where(top, qt, 0.0).astype(BF16))
        qts.append(jnp.where(top, 0.0, qt).astype(BF16))
    m_scr[...] = jnp.full(m_scr.shape, -jnp.inf, F32)
    acc_scr[...] = jnp.zeros(acc_scr.shape, F32)
    one = jnp.ones((), BF16)

    def scores(chunks):
        kcs = [k_ref[c * tk:(c + 1) * tk, :] for c in chunks]
        return [[jnp.dot(kc, qts[j], preferred_element_type=F32) for kc in kcs] for j in range(4)]

    def finish(chunks, sts):
        vts = [(jnp.where(row_v, vt_ref[c], one), jnp.where(row_v, one, vt_ref[c])) for c in chunks]
        ps, alphas = [], []
        for j in range(4):
            m_old = m_scr[j]
            m_new = m_old
            for st in sts[j]:
                m_new = jnp.maximum(m_new, jnp.max(st, axis=0, keepdims=True))
            ps.append([jnp.exp(st - m_new).astype(BF16) for st in sts[j]])
            alphas.append(jnp.exp(m_old - m_new))
            m_scr[j] = m_new
        pvs = [sum(jnp.dot(vts[n][j % 2], ps[j][n], preferred_element_type=F32) for n in range(len(chunks)))
               for j in range(4)]
        for j in range(4):
            acc_scr[j] = alphas[j] * acc_scr[j] + pvs[j]

    def attend(groups):
        sts = scores(groups[0])
        for g in range(len(groups)):
            nxt = scores(groups[g + 1]) if g + 1 < len(groups) else None
            finish(groups[g], sts)
            sts = nxt

    group = 4
    first = group + n_kv_all % group if n_kv_all > group else n_kv_all
    all_groups = [list(range(first))] + [list(range(s, s + group)) for s in range(first, n_kv_all, group)]

    @pl.when(i < n_ctx_blocks)
    def _():
        attend([list(range(n_kv_ctx))])

    @pl.when(i >= n_ctx_blocks)
    def _():
        attend(all_groups)

    for blk in range(2):
        a0, a1 = acc_scr[2 * blk], acc_scr[2 * blk + 1]
        ot = jnp.where(top, a0 / a0[HEAD_DIM:HEAD_DIM + 1, :], a1 / a1[0:1, :])
        o_ref[:, blk * LANES:(blk + 1) * LANES] = ot.T.astype(BF16)


def _global_attn(qkv, vt, n_ctx):
    b, t, _ = qkv.shape
    tq = min(256, n_ctx)
    tk = vt.shape[3]
    kern = functools.partial(_global_attn_kernel, tk=tk, n_ctx_blocks=n_ctx // tq,
                             n_kv_ctx=n_ctx // tk, n_kv_all=t // tk)
    return pl.pallas_call(
        kern,
        grid=(b, t // tq),
        in_specs=[pl.BlockSpec((None, tq, 2 * LANES), lambda bi, i: (bi, i, 0)),
                  pl.BlockSpec((None, t, LANES), lambda bi, i: (bi, 0, 2)),
                  pl.BlockSpec((None, t // tk, LANES, tk), lambda bi, i: (bi, 0, 0, 0))],
        out_specs=pl.BlockSpec((None, tq, 2 * LANES), lambda bi, i: (bi, i, 0)),
        out_shape=jax.ShapeDtypeStruct((b, t, 2 * LANES), BF16),
        scratch_shapes=[pltpu.VMEM((4, 1, tq), F32), pltpu.VMEM((4, LANES, tq), F32)],
        compiler_params=_cparams("parallel", "arbitrary"),
        name="global_attn",
    )(qkv, qkv, vt)


def _window_attn_kernel(sink_ref, q_ref, k_ref, v_ref, o_ref, *, n_ctx, seq):
    i = pl.program_id(1)
    qb = Q_BLOCK
    band = 3 * qb
    n_ctx_blocks = n_ctx // qb
    is_lat = i >= n_ctx_blocks
    n = i - n_ctx_blocks
    start = jnp.clip((n - 1) * qb, 0, seq - band)
    off = pl.multiple_of(start + n_ctx, qb)
    kc, vc = k_ref[0:n_ctx, :], v_ref[0:n_ctx, :]
    kb, vb = k_ref[pl.ds(off, band), :], v_ref[pl.ds(off, band), :]
    qpos = n * qb + lax.broadcasted_iota(jnp.int32, (qb, band), 0)
    kpos = start + lax.broadcasted_iota(jnp.int32, (qb, band), 1)
    near = (jnp.abs(kpos - qpos) <= WINDOW) & is_lat
    lane = lax.broadcasted_iota(jnp.int32, (qb, LANES), 1)
    low = lane < HEAD_DIM
    zero = jnp.zeros((), BF16)
    qs = []
    for j in range(4):
        blk = q_ref[:, (j // 2) * LANES:(j // 2 + 1) * LANES]
        qs.append(jnp.where(low if j % 2 == 0 else ~low, blk, zero))
    scores = [(_dot_nt(q, kc), _dot_nt(q, kb)) for q in qs]
    probs, dens = [], []
    for j in range(4):
        sink = sink_ref[j]
        s_c = scores[j][0]
        s_b = jnp.where(near, scores[j][1], -jnp.inf)
        m = jnp.maximum(jnp.maximum(jnp.max(s_c, axis=-1, keepdims=True), jnp.max(s_b, axis=-1, keepdims=True)), sink)
        p_c = jnp.exp(s_c - m)
        p_b = jnp.exp(s_b - m)
        dens.append(jnp.sum(p_c, axis=-1, keepdims=True) + jnp.sum(p_b, axis=-1, keepdims=True) + jnp.exp(sink - m))
        probs.append((p_c.astype(BF16), p_b.astype(BF16)))
    accs = [jnp.dot(p_c, vc, preferred_element_type=F32) + jnp.dot(p_b, vb, preferred_element_type=F32)
            for p_c, p_b in probs]
    outs = [acc / den for acc, den in zip(accs, dens)]
    for blk in range(2):
        o_ref[:, blk * LANES:(blk + 1) * LANES] = jnp.where(low, outs[2 * blk], outs[2 * blk + 1]).astype(BF16)


def _window_attn(qkv, sink, n_ctx):
    b, t, _ = qkv.shape
    kern = functools.partial(_window_attn_kernel, n_ctx=n_ctx, seq=t - n_ctx)
    return pl.pallas_call(
        kern,
        grid_spec=pltpu.PrefetchScalarGridSpec(
            num_scalar_prefetch=1,
            grid=(b, t // Q_BLOCK),
            in_specs=[pl.BlockSpec((None, Q_BLOCK, 2 * LANES), lambda bi, i, s: (bi, i, 2)),
                      pl.BlockSpec((None, t, LANES), lambda bi, i, s: (bi, 0, 6)),
                      pl.BlockSpec((None, t, LANES), lambda bi, i, s: (bi, 0, 7))],
            out_specs=pl.BlockSpec((None, Q_BLOCK, 2 * LANES), lambda bi, i, s: (bi, i, 0)),
        ),
        out_shape=jax.ShapeDtypeStruct((b, t, 2 * LANES), BF16),
        compiler_params=_cparams("parallel", "arbitrary"),
        name="window_attn",
    )(sink, qkv, qkv, qkv)


_HEAD_PERM = np.concatenate([np.arange(0, HEAD_DIM, 2), np.arange(1, HEAD_DIM, 2)])
_Q_HEAD_ORDER = (0, 2, 1, 3)


def _q_cols(base, rotary):
    inner = _HEAD_PERM if rotary else np.arange(HEAD_DIM)
    return np.concatenate([base + h * HEAD_DIM + inner for h in _Q_HEAD_ORDER])


def _kv_cols(base, rotary):
    inner = _HEAD_PERM if rotary else np.arange(HEAD_DIM)
    return np.concatenate([base + h * HEAD_DIM + inner for h in range(ATT_KV_HEADS)])


_ATT_COL_PERM = np.concatenate([
    _q_cols(0, True), _kv_cols(256, True), _kv_cols(384, False),
    _q_cols(512, True), _kv_cols(768, True), _kv_cols(896, False)])
_OUT_ROW_PERM = np.concatenate([_q_cols(0, False), _q_cols(256, False), np.arange(512, 1024)])


def _permute_att_cols(w):
    d = w.shape[0]
    half = HEAD_DIM // 2

    def q_part(x):
        return x.reshape(d, 2, 2, half, 2).transpose(0, 2, 1, 4, 3).reshape(d, ATT_HEADS * HEAD_DIM)

    def k_part(x):
        return x.reshape(d, ATT_KV_HEADS, half, 2).transpose(0, 1, 3, 2).reshape(d, ATT_KV_HEADS * HEAD_DIM)

    return jnp.concatenate([q_part(w[:, 0:256]), k_part(w[:, 256:384]), w[:, 384:512],
                            q_part(w[:, 512:768]), k_part(w[:, 768:896]), w[:, 896:1024]], axis=1)


def _permute_out_rows(w):
    d = w.shape[1]
    q_part = lambda x: x.reshape(2, 2, HEAD_DIM, d).transpose(1, 0, 2, 3).reshape(ATT_HEADS * HEAD_DIM, d)
    return jnp.concatenate([q_part(w[0:256]), q_part(w[256:512]), w[512:]], axis=0)


def _rope_tables(n_ctx, seq):
    rows = seq // GRID_W
    row = jnp.repeat(jnp.arange(rows, dtype=F32), GRID_W)
    col = jnp.tile(jnp.arange(GRID_W, dtype=F32), rows)
    n_freq = HEAD_DIM // 4
    inv = ROPE_THETA ** (-jnp.arange(n_freq, dtype=F32) / n_freq)
    ang = jnp.concatenate([row[:, None] * inv, col[:, None] * inv], axis=-1)
    cos = jnp.concatenate([jnp.ones((n_ctx, HEAD_DIM // 2), F32), jnp.cos(ang)], axis=0)
    sin = jnp.concatenate([jnp.zeros((n_ctx, HEAD_DIM // 2), F32), jnp.sin(ang)], axis=0)
    cos_t = jnp.tile(jnp.concatenate([cos, cos], axis=-1), (1, LANES // HEAD_DIM))
    sin_t = jnp.tile(jnp.concatenate([-sin, sin], axis=-1), (1, LANES // HEAD_DIM))
    return cos_t, sin_t


def _rwkv_prep_kernel(cur_ref, prev_ref, next_ref, conv_ref, w0_ref, w2_ref, a0_ref, a2_ref, g2_ref,
                      kk_ref, ka_ref, rk_ref, grp_ref,
                      r_out, v_out, al_out, lw_out, kd_out, be_out, bonus_out, gate_out, *, n_ctx_tiles, n_tiles):
    i = pl.program_id(1)
    tt = cur_ref.shape[0]
    c = RWKV_WIDTH
    x = cur_ref[...]
    row = lax.broadcasted_iota(jnp.int32, (tt, 1), 0)
    has_prev = jnp.logical_and(i != 0, i != n_ctx_tiles)
    has_next = jnp.logical_and(i != n_ctx_tiles - 1, i != n_tiles - 1)
    halo_prev = jnp.where(has_prev, prev_ref[7:8, :], 0.0)
    halo_next = jnp.where(has_next, next_ref[0:1, :], 0.0)
    x_prev = jnp.where(row == 0, halo_prev, pltpu.roll(x, 1, 0))
    x_next = jnp.where(row == tt - 1, halo_next, pltpu.roll(x, tt - 1, 0))
    t = x_prev * conv_ref[0:1, :] + x * conv_ref[1:2, :] + x_next * conv_ref[2:3, :]
    r, k, v = t[:, 0:c], t[:, c:2 * c], t[:, 2 * c:3 * c]
    wd = t[:, 3 * c:3 * c + 2 * DECAY_LORA]
    ad = t[:, 3 * c + 2 * DECAY_LORA:3 * c + 2 * DECAY_LORA + 2 * ICLR_LORA]
    gd = t[:, 3 * c + 2 * DECAY_LORA + 2 * ICLR_LORA:]
    z = w0_ref[...] + _dot_x3(jnp.tanh(wd), w2_ref[...])
    neg_softplus = -(jnp.maximum(-z, 0.0) + jnp.log(1.0 + jnp.exp(-jnp.abs(z))))
    lw_out[...] = -jnp.exp(neg_softplus - 0.5)
    a = _sigmoid(a0_ref[...] + _dot(ad, a2_ref[...]))
    grp = grp_ref[...]
    kk = k * kk_ref[...]
    ss = jnp.concatenate([_dot_exact_rhs(kk[:, j * LANES:(j + 1) * LANES] * kk[:, j * LANES:(j + 1) * LANES], grp)
                          for j in range(c // LANES)], axis=-1)
    kk = kk / jnp.maximum(jnp.sqrt(ss), 1e-12)
    bonus_dot = jnp.zeros((tt, c), F32)
    for j in range(2):
        a_j = a[:, j * c:(j + 1) * c]
        kd_j = k * (1.0 + (a_j - 1.0) * ka_ref[...])
        kd_out[:, j * c:(j + 1) * c] = kd_j
        be_out[:, j * c:(j + 1) * c] = a_j * kk
        bonus_dot = bonus_dot + r * kd_j * rk_ref[j:j + 1, :]
    bsum = jnp.concatenate([_dot_exact_rhs(bonus_dot[:, j * LANES:(j + 1) * LANES], grp)
                            for j in range(c // LANES)], axis=-1)
    r_out[...] = r
    v_out[...] = v
    al_out[...] = kk
    bonus_out[...] = bsum * v
    gate_out[...] = _dot(_sigmoid(gd), g2_ref[...])


def _rwkv_prep(ur, conv, w0, w2blk, a0, a2blk, g2, k_k, k_a, r_k, grp, n_ctx):
    b, t, cols = ur.shape
    c = RWKV_WIDTH
    tt = min(256, n_ctx)
    n_tiles = t // tt
    per8 = tt // 8
    kern = functools.partial(_rwkv_prep_kernel, n_ctx_tiles=n_ctx // tt, n_tiles=n_tiles)
    const = lambda shape: pl.BlockSpec(shape, lambda bi, i: (0,) * len(shape))
    tok = lambda w: pl.BlockSpec((None, tt, w), lambda bi, i: (bi, i, 0))
    return pl.pallas_call(
        kern,
        grid=(b, n_tiles),
        in_specs=[tok(cols),
                  pl.BlockSpec((None, 8, cols), lambda bi, i: (bi, jnp.maximum(i * per8 - 1, 0), 0)),
                  pl.BlockSpec((None, 8, cols), lambda bi, i: (bi, jnp.minimum((i + 1) * per8, t // 8 - 1), 0)),
                  const((3, cols)), const((1, 2 * c)), const((2 * DECAY_LORA, 2 * c)), const((1, 2 * c)),
                  const((2 * ICLR_LORA, 2 * c)), const((GATE_LORA, c)), const((1, c)), const((1, c)),
                  const((2, c)), const((LANES, LANES))],
        out_specs=[tok(c), tok(c), tok(c), tok(2 * c), tok(2 * c), tok(2 * c), tok(c), tok(c)],
        out_shape=[jax.ShapeDtypeStruct((b, t, w), F32) for w in (c, c, c, 2 * c, 2 * c, 2 * c, c, c)],
        compiler_params=_cparams("parallel", "parallel"),
        name="rwkv_prep",
    )(ur, ur, ur, conv, w0, w2blk, a0, a2blk, g2, k_k, k_a, r_k, grp)


RWKV_CHUNK = 64
INV_BLOCK = 16


def _rwkv_masks():
    n2 = 2 * RWKV_CHUNK
    rr, cc = np.indices((n2, n2))
    same = (rr // RWKV_CHUNK) == (cc // RWKV_CHUNK)
    masks = np.stack([same & (cc < rr), same & (cc <= rr), same & (cc > rr), same & (cc >= rr),
                      (rr // INV_BLOCK) == (cc // INV_BLOCK), rr == cc]).astype(np.float32)
    tr, tc = np.indices((RWKV_CHUNK, RWKV_CHUNK))
    tri = np.stack([tc <= tr, tc >= tr]).astype(np.float32)
    return jnp.asarray(masks), jnp.asarray(tri, BF16)


def _rwkv_scan_kernel(rf_ref, vf_ref, alf_ref, lwf_ref, kdf_ref, bef_ref,
                      rb_ref, vb_ref, alb_ref, lwb_ref, kdb_ref, beb_ref,
                      mask_ref, tri_ref, of_ref, ob_ref, s_scr):
    cs = RWKV_CHUNK

    @pl.when(pl.program_id(1) == 0)
    def _():
        s_scr[...] = jnp.zeros(s_scr.shape, F32)

    blk16, eye = mask_ref[4], mask_ref[5]
    low = lax.broadcasted_iota(jnp.int32, (cs, LANES), 1) < HEAD_DIM
    bf = lambda x: x.astype(BF16)
    mm = lambda a, b: jnp.dot(a, b, preferred_element_type=F32)
    nt = lambda a, b: lax.dot_general(a, b, (((1,), (1,)), ((), ())), preferred_element_type=F32)
    tn = lambda a, b: lax.dot_general(a, b, (((0,), (0,)), ((), ())), preferred_element_type=F32)

    def stack(x):
        return jnp.concatenate([jnp.where(low, x, 0.0), jnp.where(low, 0.0, x)], axis=0)

    chains = []
    for d, (r_ref, v_ref, al_ref, lw_ref, kd_ref, be_ref, o_ref) in enumerate((
            (rf_ref, vf_ref, alf_ref, lwf_ref, kdf_ref, bef_ref, of_ref),
            (rb_ref, vb_ref, alb_ref, lwb_ref, kdb_ref, beb_ref, ob_ref))):
        lw = lw_ref[...]
        lw_hi, lw_lo = _split_hi_lo(lw)
        tri = tri_ref[d]
        cl = mm(tri, lw_hi) + mm(tri, lw_lo)
        tot = jnp.sum(lw, axis=0, keepdims=True)
        e_neg, e_end = jnp.exp(-cl), jnp.exp(tot - cl)
        a_t = al_ref[...] * jnp.exp(cl - lw)
        r_t = r_ref[...] * jnp.exp(cl)
        k_t, b_t = kd_ref[...] * e_neg, be_ref[...] * e_neg
        k_e, b_e = kd_ref[...] * e_end, be_ref[...] * e_end
        p_end = jnp.exp(tot)
        v = v_ref[...]
        for p in range(RWKV_HEADS // 2):
            sl = slice(p * LANES, (p + 1) * LANES)
            xa, xr = stack(a_t[:, sl]), stack(r_t[:, sl])
            chains.append(dict(
                d=d, p=p, sl=sl, o_ref=o_ref, strict=mask_ref[2 * d], incl=mask_ref[2 * d + 1],
                xa=xa, xr=xr, lhs=bf(jnp.concatenate([xa, xr], axis=0)),
                ybk=bf(jnp.concatenate([b_t[:, sl], b_t[:, sl], k_t[:, sl], k_t[:, sl]], axis=0)),
                vs=bf(stack(v[:, sl])), kbs=bf(jnp.concatenate([stack(k_e[:, sl]), stack(b_e[:, sl])], axis=0)),
                pe=p_end[:, sl]))
    n2 = 2 * cs
    for c in chains:
        c["x"] = nt(c["lhs"], c["ybk"])
    for c in chains:
        x = c["x"]
        l_m = x[:n2, :n2] * c["strict"]
        c["m_ak"] = bf(x[:n2, n2:] * c["strict"])
        c["m_r"] = bf(jnp.concatenate([x[n2:, n2:] * c["incl"], x[n2:, :n2] * c["incl"]], axis=1))
        l_d = l_m * blk16
        c["l_d"], c["l_o"] = l_d, bf(l_m - l_d)
    for c in chains:
        ld = bf(c["l_d"])
        c["l2"] = mm(ld, ld)
        c["mv"] = mm(c["m_ak"], c["vs"])
    for c in chains:
        l2 = bf(c["l2"])
        t1 = eye - c["l_d"]
        y = mm(l2, jnp.concatenate([l2, bf(t1)], axis=1))
        c["l4"], c["t2"] = bf(y[:, :n2]), t1 + y[:, n2:]
    for c in chains:
        y = mm(c["l4"], jnp.concatenate([c["l4"], bf(c["t2"])], axis=1))
        c["l8"], c["t4"] = bf(y[:, :n2]), c["t2"] + y[:, n2:]
    for c in chains:
        c["dinv"] = bf(c["t4"] + mm(c["l8"], bf(c["t4"])))
    for c in chains:
        y = mm(c["dinv"], jnp.concatenate([c["l_o"], bf(c["mv"])], axis=1))
        c["n"] = bf(y[:, :n2])
        c["dr"] = jnp.concatenate([y[:, n2:], mm(c["dinv"], bf(c["xa"]))], axis=1)
    for c in chains:
        dr = bf(c["dr"])
        y = mm(c["n"], jnp.concatenate([c["n"], dr[:, :n2]], axis=1))
        c["nsq"] = bf(y[:, :n2])
        c["xx"] = c["dr"] - jnp.concatenate([y[:, n2:], mm(c["n"], dr[:, n2:])], axis=1)
    for c in chains:
        c["uw"] = bf(c["xx"] + mm(c["nsq"], bf(c["xx"])))
    for c in chains:
        rhs = jnp.concatenate([jnp.concatenate([c["vs"], jnp.zeros_like(c["vs"])], axis=1), -c["uw"]], axis=0)
        y = mm(c["m_r"], rhs)
        g = tn(c["kbs"], rhs)
        rqs = c["xr"] + y[:, n2:]
        c["rq"] = bf(rqs[:cs] + rqs[cs:])
        c["y0"] = y[:cs, :n2] + y[cs:, :n2]
        c["m_t"] = bf(eye * c["pe"] + g[:, n2:])
        c["g_t"] = g[:, :n2]
    for c in chains:
        s_old = bf(s_scr[c["d"], c["p"]])
        c["o_ref"][:, c["sl"]] = mm(c["rq"], s_old) + c["y0"]
        s_scr[c["d"], c["p"]] = mm(c["m_t"], s_old) + c["g_t"]


def _rwkv_scan(r, v, al, lw, kd, be, n_ctx):
    b, t, c = r.shape
    cs = RWKV_CHUNK
    ncc, ntot = n_ctx // cs, t // cs
    masks, tri = _rwkv_masks()

    def back(s):
        return jnp.where(s < ncc, ncc - 1 - s, ntot - 1 - s + ncc)

    fwd = lambda col: pl.BlockSpec((None, cs, c), lambda bi, s: (bi, s, col))
    bwd = lambda col: pl.BlockSpec((None, cs, c), lambda bi, s: (bi, back(s), col))
    const = lambda a: pl.BlockSpec(a.shape, lambda bi, s: (0,) * a.ndim)
    return pl.pallas_call(
        _rwkv_scan_kernel,
        grid=(b, ntot),
        in_specs=[fwd(0), fwd(0), fwd(0), fwd(0), fwd(0), fwd(0),
                  bwd(0), bwd(0), bwd(0), bwd(1), bwd(1), bwd(1), const(masks), const(tri)],
        out_specs=[pl.BlockSpec((None, cs, c), lambda bi, s: (bi, s, 0)),
                   pl.BlockSpec((None, cs, c), lambda bi, s: (bi, back(s), 0))],
        out_shape=[jax.ShapeDtypeStruct((b, t, c), F32), jax.ShapeDtypeStruct((b, t, c), F32)],
        scratch_shapes=[pltpu.VMEM((2, RWKV_HEADS // 2, LANES, LANES), F32)],
        compiler_params=_cparams("parallel", "arbitrary"),
        name="rwkv_scan",
    )(r, v, al, lw, kd, be, r, v, al, lw, kd, be, masks, tri)


def _rwkv_output(of_ref, ob_ref, bonus_ref, gate_ref, lnw_ref, lnb_ref, grp):
    outs = []
    for j in range(RWKV_WIDTH // LANES):
        sl = slice(j * LANES, (j + 1) * LANES)
        o = of_ref[:, sl] + ob_ref[:, sl]
        mu = _dot_exact_rhs(o, grp) * (1.0 / HEAD_DIM)
        d = o - mu
        var = _dot_exact_rhs(d * d, grp) * (1.0 / HEAD_DIM)
        on = d * lax.rsqrt(var + GN_EPS) * lnw_ref[:, sl] + lnb_ref[:, sl]
        outs.append(((on + bonus_ref[:, sl]) * gate_ref[:, sl]).astype(BF16))
    return outs


def _block_diag2(w):
    z = jnp.zeros_like(w[0])
    return jnp.concatenate([jnp.concatenate([w[0], z], axis=1), jnp.concatenate([z, w[1]], axis=1)], axis=0)


def _proj_out_kernel(hc_ref, hx_ref, oa_ref, ow_ref, of_ref, ob_ref, bonus_ref, gate_ref, lnw_ref, lnb_ref, grp_ref,
                     w_ref, mod_ref, g_ref, hn_ref, f_ref, *, n_ctx_tiles):
    na, nw = oa_ref.shape[1], ow_ref.shape[1]
    mix = (jnp.dot(oa_ref[...], w_ref[0:na, :], preferred_element_type=F32)
           + jnp.dot(ow_ref[...], w_ref[na:na + nw, :], preferred_element_type=F32))
    o_r = _rwkv_output(of_ref, ob_ref, bonus_ref, gate_ref, lnw_ref, lnb_ref, grp_ref[...])
    for j, o_j in enumerate(o_r):
        base = na + nw + j * LANES
        mix = mix + jnp.dot(o_j, w_ref[base:base + LANES, :], preferred_element_type=F32)
    h = _stream_tile(hc_ref, hx_ref, n_ctx_tiles) + mod_ref[2:3, :] * mix
    hn_ref[...] = h
    y = h * lax.rsqrt(jnp.mean(h * h, axis=-1, keepdims=True) + NORM_EPS) * g_ref[...]
    f_ref[...] = y * (1.0 + mod_ref[4:5, :]) + mod_ref[3:4, :]


def _proj_out(stream, o_a, o_w, o_f, o_b, bonus, gate, ln_w, ln_b, grp, w_out, modsel, g, n_ctx):
    hc, hx, x_off = stream
    b, t, c = o_f.shape
    d = hc.shape[2]
    tm = min(256, n_ctx)
    nct = n_ctx // tm
    tok = lambda w: pl.BlockSpec((None, tm, w), lambda bi, i: (bi, i, 0))
    const = lambda shape: pl.BlockSpec(shape, lambda bi, i: (0,) * len(shape))
    return pl.pallas_call(
        functools.partial(_proj_out_kernel, n_ctx_tiles=nct),
        grid=(b, t // tm),
        in_specs=_stream_specs(tm, d, nct, x_off) + [
                  tok(o_a.shape[2]), tok(o_w.shape[2]), tok(c), tok(c), tok(c), tok(c),
                  const((1, c)), const((1, c)), const((LANES, LANES)),
                  pl.BlockSpec(w_out.shape, lambda bi, i: (0, 0)),
                  pl.BlockSpec((None, None, N_MOD, d), lambda bi, i: (bi, jnp.where(i >= nct, 1, 0), 0, 0)),
                  pl.BlockSpec((1, d), lambda bi, i: (0, 0))],
        out_specs=[tok(d), tok(d)],
        out_shape=[jax.ShapeDtypeStruct((b, t, d), F32), jax.ShapeDtypeStruct((b, t, d), F32)],
        compiler_params=_cparams("parallel", "parallel"),
        name="proj_out",
    )(hc, hx, o_a, o_w, o_f, o_b, bonus, gate, ln_w, ln_b, grp, w_out, modsel, g.reshape(1, d))


PAIRS_PER_GROUP = 6
N_PAIR_CLASSES = N_EXPERT_GROUPS * PAIRS_PER_GROUP
_PAIR_LO = (0, 0, 0, 1, 1, 2)
_PAIR_HI = (1, 2, 3, 2, 3, 3)


def _router_kernel(f_ref, wr_ref, bias_ref, cls_ref, gate_ref, rank_ref, cnt_ref):
    tt = f_ref.shape[0]
    per_group = N_EXPERTS // N_EXPERT_GROUPS

    @pl.when(pl.program_id(0) == 0)
    def _():
        cnt_ref[...] = jnp.zeros(cnt_ref.shape, F32)

    logits = _dot_nt(wr_ref[...], f_ref[...])
    score = _sigmoid(logits)
    biased = score + bias_ref[...]
    b = [biased[e:e + 1, :] for e in range(N_EXPERTS)]
    s = [score[e:e + 1, :] for e in range(N_EXPERTS)]
    g_best, g_sel = None, None
    for g in range(N_EXPERT_GROUPS):
        rows = b[g * per_group:(g + 1) * per_group]
        top2 = None
        for i in range(per_group):
            for j in range(i + 1, per_group):
                pair = rows[i] + rows[j]
                top2 = pair if top2 is None else jnp.maximum(top2, pair)
        if g == 0:
            g_best, g_sel = top2, jnp.zeros(top2.shape, jnp.int32)
        else:
            upd = top2 > g_best
            g_best = jnp.where(upd, top2, g_best)
            g_sel = jnp.where(upd, g, g_sel)
    neg = jnp.full(g_best.shape, -jnp.inf, F32)
    m = [jnp.where(g_sel == e // per_group, b[e], neg) for e in range(N_EXPERTS)]

    def first_argmax(vals):
        best, idx = vals[0], jnp.zeros(vals[0].shape, jnp.int32)
        for e in range(1, N_EXPERTS):
            upd = vals[e] > best
            best = jnp.where(upd, vals[e], best)
            idx = jnp.where(upd, e, idx)
        return idx

    e1 = first_argmax(m)
    e2 = first_argmax([jnp.where(e1 == e, neg, m[e]) for e in range(N_EXPERTS)])
    zero = jnp.zeros(g_best.shape, F32)
    s1 = sum(jnp.where(e1 == e, s[e], zero) for e in range(N_EXPERTS))
    s2 = sum(jnp.where(e2 == e, s[e], zero) for e in range(N_EXPERTS))
    den = s1 + s2
    lo, hi = jnp.minimum(e1, e2), jnp.maximum(e1, e2)
    lo_l, hi_l = lo - g_sel * per_group, hi - g_sel * per_group
    pair = jnp.where(lo_l == 0, hi_l - 1, jnp.where(lo_l == 1, hi_l + 1, 5))
    cls = g_sel * PAIRS_PER_GROUP + pair
    hits = jnp.concatenate([jnp.where(cls == k, 1.0, 0.0) for k in range(N_PAIR_CLASSES)], axis=0)
    tr = lax.broadcasted_iota(jnp.int32, (tt, tt), 0)
    tc = lax.broadcasted_iota(jnp.int32, (tt, tt), 1)
    earlier = jnp.where(tr < tc, 1.0, 0.0).astype(BF16)
    before = cnt_ref[:, 0:1] + jnp.dot(hits.astype(BF16), earlier, preferred_element_type=F32)
    cls_ref[...] = cls
    first_is_lo = e1 < e2
    gate_ref[0:1, :] = jnp.where(first_is_lo, s1, s2) / den
    gate_ref[1:2, :] = jnp.where(first_is_lo, s2, s1) / den
    rank_ref[...] = jnp.sum(hits * before, axis=0, keepdims=True).astype(jnp.int32)
    cnt_ref[...] = cnt_ref[...] + jnp.sum(hits, axis=1, keepdims=True)


def _router(f2, wr_t, bias):
    n, d = f2.shape
    tt = int(np.gcd(n, 512))
    row = lambda r: pl.BlockSpec((r, tt), lambda i: (0, i))
    return pl.pallas_call(
        _router_kernel,
        grid=(n // tt,),
        in_specs=[pl.BlockSpec((tt, d), lambda i: (i, 0)),
                  pl.BlockSpec((N_EXPERTS, d), lambda i: (0, 0)),
                  pl.BlockSpec((N_EXPERTS, 1), lambda i: (0, 0))],
        out_specs=[row(1), row(2), row(1), pl.BlockSpec((N_PAIR_CLASSES, LANES), lambda i: (0, 0))],
        out_shape=[jax.ShapeDtypeStruct((1, n), jnp.int32), jax.ShapeDtypeStruct((2, n), F32),
                   jax.ShapeDtypeStruct((1, n), jnp.int32), jax.ShapeDtypeStruct((N_PAIR_CLASSES, LANES), F32)],
        compiler_params=_cparams("arbitrary"),
        name="router",
    )(f2, wr_t, bias)


MOE_BLOCK = 256
DMA_UNROLL = 16


def _dispatch_kernel(slot_ref, f_ref, init_hbm, xs_hbm, sem):
    del init_hbm
    td = slot_ref.shape[1]

    def start(r, carry):
        pltpu.make_async_copy(f_ref.at[pl.ds(r, 1)], xs_hbm.at[pl.ds(slot_ref[0, r], 1)], sem).start()
        return carry

    lax.fori_loop(0, td, start, 0, unroll=DMA_UNROLL)
    pltpu.make_async_copy(f_ref, xs_hbm.at[pl.ds(0, td)], sem).wait()


def _dispatch(f2, slot, n_slots):
    n, d = f2.shape
    td = int(np.gcd(n, 1024))
    return pl.pallas_call(
        _dispatch_kernel,
        grid=(n // td,),
        in_specs=[pl.BlockSpec((1, td), lambda i: (0, i), memory_space=pltpu.SMEM),
                  pl.BlockSpec((td, d), lambda i: (i, 0)),
                  pl.BlockSpec(memory_space=pl.ANY)],
        out_specs=pl.BlockSpec(memory_space=pl.ANY),
        out_shape=jax.ShapeDtypeStruct((n_slots, d), F32),
        scratch_shapes=[pltpu.SemaphoreType.DMA(())],
        input_output_aliases={2: 0},
        compiler_params=_cparams("arbitrary"),
        name="moe_dispatch",
    )(slot, f2, jnp.zeros((n_slots, d), F32))


def _ffn_kernel(e_lo_ref, e_hi_ref, n_used_ref, x_ref, wg0_ref, wu0_ref, wd0_ref, wg1_ref, wu1_ref, wd1_ref, y_ref):
    del e_lo_ref, e_hi_ref
    d = x_ref.shape[1]

    @pl.when(pl.program_id(0) < n_used_ref[0])
    def _():
        x = x_ref[...].astype(BF16)
        for k, (wg_ref, wu_ref, wd_ref) in enumerate(((wg0_ref, wu0_ref, wd0_ref), (wg1_ref, wu1_ref, wd1_ref))):
            hg = jnp.dot(x, wg_ref[...], preferred_element_type=F32)
            hu = jnp.dot(x, wu_ref[...], preferred_element_type=F32)
            act = (hg * _sigmoid(hg) * hu).astype(BF16)
            y_ref[:, k * d:(k + 1) * d] = jnp.dot(act, wd_ref[...], preferred_element_type=F32)

    @pl.when(pl.program_id(0) >= n_used_ref[0])
    def _():
        y_ref[...] = jnp.zeros(y_ref.shape, F32)


def _expert_ffn(xs, e_lo, e_hi, n_used, wg, wu, wd):
    n_slots, d = xs.shape
    ff = wg.shape[2]
    bm = MOE_BLOCK
    last = lambda j, nu: jnp.minimum(j, nu[0] - 1)
    w_in = lambda which: pl.BlockSpec((None, d, ff), lambda j, lo, hi, nu: ((lo, hi)[which][last(j, nu)], 0, 0))
    w_out = lambda which: pl.BlockSpec((None, ff, d), lambda j, lo, hi, nu: ((lo, hi)[which][last(j, nu)], 0, 0))
    return pl.pallas_call(
        _ffn_kernel,
        grid_spec=pltpu.PrefetchScalarGridSpec(
            num_scalar_prefetch=3,
            grid=(n_slots // bm,),
            in_specs=[pl.BlockSpec((bm, d), lambda j, lo, hi, nu: (last(j, nu), 0)),
                      w_in(0), w_in(0), w_out(0), w_in(1), w_in(1), w_out(1)],
            out_specs=pl.BlockSpec((bm, 2 * d), lambda j, lo, hi, nu: (j, 0)),
        ),
        out_shape=jax.ShapeDtypeStruct((n_slots, 2 * d), F32),
        compiler_params=_cparams("arbitrary"),
        name="expert_ffn",
    )(e_lo, e_hi, n_used, xs, wg, wu, wd, wg, wu, wd)


def _combine_kernel(slot_ref, ys_hbm, h_ref, gate_ref, mod_ref, o_ref, buf, sem):
    tc, d = h_ref.shape

    def start(r, carry):
        pltpu.make_async_copy(ys_hbm.at[pl.ds(slot_ref[0, r], 1)], buf.at[pl.ds(r, 1)], sem).start()
        return carry

    lax.fori_loop(0, tc, start, 0, unroll=DMA_UNROLL)
    pltpu.make_async_copy(ys_hbm.at[pl.ds(0, tc)], buf, sem).wait()
    y = gate_ref[:, 0:1] * buf[:, 0:d] + gate_ref[:, 1:2] * buf[:, d:2 * d]
    o_ref[...] = h_ref[...] + mod_ref[5:6, :] * y


def _combine(ys, slot, gates_t, h, modsel, n_ctx):
    b, t, d = h.shape
    tc = min(256, n_ctx)
    nct = n_ctx // tc
    per_b = t // tc
    return pl.pallas_call(
        _combine_kernel,
        grid=(b, per_b),
        in_specs=[pl.BlockSpec((1, tc), lambda bi, i: (0, bi * per_b + i), memory_space=pltpu.SMEM),
                  pl.BlockSpec(memory_space=pl.ANY),
                  pl.BlockSpec((None, tc, d), lambda bi, i: (bi, i, 0)),
                  pl.BlockSpec((tc, 2), lambda bi, i: (bi * per_b + i, 0)),
                  pl.BlockSpec((None, None, N_MOD, d), lambda bi, i: (bi, jnp.where(i >= nct, 1, 0), 0, 0))],
        out_specs=pl.BlockSpec((None, tc, d), lambda bi, i: (bi, i, 0)),
        out_shape=jax.ShapeDtypeStruct((b, t, d), F32),
        scratch_shapes=[pltpu.VMEM((tc, 2 * d), F32), pltpu.SemaphoreType.DMA(())],
        compiler_params=_cparams("arbitrary", "arbitrary"),
        name="moe_combine",
    )(slot, ys, h, gates_t, modsel)


def _moe(f, h, modsel, wr_t, bias, wg, wu, wd, n_ctx):
    b, t, d = f.shape
    n = b * t
    f2 = f.reshape(n, d)
    cls, gates, rank, cnt = _router(f2, wr_t, bias)
    bm = MOE_BLOCK
    counts = cnt[:, 0].astype(jnp.int32)
    padded = (counts + bm - 1) // bm * bm
    pend = jnp.cumsum(padded)
    pstart = pend - padded
    slot = pstart[cls] + rank
    n_slots = -(-n // bm) * bm + N_PAIR_CLASSES * bm
    n_blocks = n_slots // bm
    blk_start = jnp.arange(n_blocks, dtype=jnp.int32) * bm
    blk_cls = jnp.minimum(jnp.sum(pend[None, :] <= blk_start[:, None], axis=1), N_PAIR_CLASSES - 1).astype(jnp.int32)
    per_group = N_EXPERTS // N_EXPERT_GROUPS
    blk_grp, blk_pair = blk_cls // PAIRS_PER_GROUP, blk_cls % PAIRS_PER_GROUP
    e_lo = blk_grp * per_group + jnp.asarray(_PAIR_LO, jnp.int32)[blk_pair]
    e_hi = blk_grp * per_group + jnp.asarray(_PAIR_HI, jnp.int32)[blk_pair]
    n_used = (pend[-1:] // bm).astype(jnp.int32)
    xs = _dispatch(f2, slot, n_slots)
    ys = _expert_ffn(xs, e_lo, e_hi, n_used, wg, wu, wd)
    return _combine(ys, slot, gates.T, h, modsel, n_ctx)


def _final_norm_kernel(h_ref, g_ref, o_ref):
    x = h_ref[...]
    o_ref[...] = x * lax.rsqrt(jnp.mean(x * x, axis=-1, keepdims=True) + NORM_EPS) * g_ref[...]


def _final_norm(h, g, n_ctx):
    b, t, d = h.shape
    tm = min(256, n_ctx)
    skip = n_ctx // tm
    return pl.pallas_call(
        _final_norm_kernel,
        grid=(b, (t - n_ctx) // tm),
        in_specs=[pl.BlockSpec((None, tm, d), lambda bi, i: (bi, i + skip, 0)),
                  pl.BlockSpec((1, d), lambda bi, i: (0, 0))],
        out_specs=pl.BlockSpec((None, tm, d), lambda bi, i: (bi, i, 0)),
        out_shape=jax.ShapeDtypeStruct((b, t - n_ctx, d), F32),
        compiler_params=_cparams("parallel", "parallel"),
        name="final_norm",
    )(h, g.reshape(1, d))


def kernel(x, c, ctx, c_ctx, w_mod, b_mod, norm_mix_g, norm_ffn_g, w_in, q_norm_g, k_norm_g, sink_logit,
           rwkv_conv, rwkv_w0, rwkv_w2, rwkv_a0, rwkv_a2, rwkv_g2, rwkv_k_k, rwkv_k_a, rwkv_r_k, rwkv_ln_w,
           rwkv_ln_b, w_out, w_router, router_bias, e_gate, e_up, e_down, final_norm_g):
    b, seq, d = x.shape
    n_ctx = ctx.shape[1]
    depth = w_mod.shape[0]
    cw = RWKV_WIDTH
    t = n_ctx + seq
    stream = (ctx, x, 0)
    cond_rows = -(-(b + 1) // 8) * 8
    cond = jnp.concatenate([c, c_ctx[None, :], jnp.zeros((cond_rows - b - 1, d), F32)], axis=0)
    cos_t, sin_t = _rope_tables(n_ctx, seq)
    grp = jnp.asarray(np.kron(np.eye(LANES // HEAD_DIM), np.ones((HEAD_DIM, HEAD_DIM))), BF16)
    wr_t = w_router.T
    bias = router_bias.reshape(N_EXPERTS, 1)
    q_order = jnp.asarray(_Q_HEAD_ORDER)
    for l in range(depth):
        mods = _modulation(cond, w_mod[l], b_mod[l])
        modsel = jnp.stack([jnp.broadcast_to(mods[b], (b, N_MOD * d)), mods[:b]], axis=1).reshape(b, 2, N_MOD, d)
        wa = _permute_att_cols(w_in[l][:, :ATT_COLS]).astype(BF16)
        wr = w_in[l][:, ATT_COLS:].astype(BF16)
        qg = jnp.tile(q_norm_g[l][_HEAD_PERM], LANES // HEAD_DIM).reshape(1, LANES)
        kg = jnp.tile(k_norm_g[l][_HEAD_PERM], LANES // HEAD_DIM).reshape(1, LANES)
        qkv, vt, ur = _proj_in(stream, t, modsel, norm_mix_g[l], wa, wr, cos_t, sin_t, qg, kg, grp, n_ctx)
        o_a = _global_attn(qkv, vt, n_ctx)
        o_w = _window_attn(qkv, sink_logit[l][q_order], n_ctx)
        r_, v_, al_, lw_, kd_, be_, bonus_, gate_ = _rwkv_prep(
            ur, rwkv_conv[l], rwkv_w0[l].reshape(1, 2 * cw), _block_diag2(rwkv_w2[l]),
            rwkv_a0[l].reshape(1, 2 * cw), _block_diag2(rwkv_a2[l]).astype(BF16), rwkv_g2[l].astype(BF16),
            rwkv_k_k[l].reshape(1, cw), rwkv_k_a[l].reshape(1, cw), rwkv_r_k[l].reshape(2, cw), grp, n_ctx)
        o_f, o_b = _rwkv_scan(r_, v_, al_, lw_, kd_, be_, n_ctx)
        h, f = _proj_out(stream, o_a, o_w, o_f, o_b, bonus_, gate_, rwkv_ln_w[l].reshape(1, cw),
                         rwkv_ln_b[l].reshape(1, cw), grp, _permute_out_rows(w_out[l]).astype(BF16), modsel,
                         norm_ffn_g[l], n_ctx)
        h = _moe(f, h, modsel, wr_t, bias, e_gate[l].astype(BF16), e_up[l].astype(BF16), e_down[l].astype(BF16), n_ctx)
        stream = (h, h, n_ctx // min(256, n_ctx))
    return _final_norm(h, final_norm_g, n_ctx)
```
